```python
import math
import jax, jax.numpy as jnp
from jax import lax
import numpy as np

D_MODEL = 2048
BATCH = 4
SEQ = 2048
DEPTH = 2
DEC_BATCH = 128
DEC_SEQ = 1
PAST_LEN = 16384
PAGE_SIZE = 128

D_MIX = D_MODEL
N_BRANCH = 4
W_BRANCH = D_MIX // N_BRANCH
EPS = 1e-6
GLA_HEADS = 4
GLA_DK = W_BRANCH // 2 // GLA_HEADS
GLA_DV = W_BRANCH // GLA_HEADS
GLA_RANK = 16
GLA_TAU = 16.0
GLA_CHUNK = 64
SGU_HEADS = 4
SGU_HD = W_BRANCH // SGU_HEADS
SGU_CHUNK = 128
CONV_W = 3
SSM_GROUP_SIZE = 16
SSM_GROUPS = W_BRANCH // SSM_GROUP_SIZE
SSM_N = 64
PROJ_SIZES = (GLA_HEADS * GLA_DK, GLA_HEADS * GLA_DK, W_BRANCH, GLA_RANK, W_BRANCH,
              W_BRANCH, W_BRANCH, W_BRANCH,
              W_BRANCH, W_BRANCH, W_BRANCH, W_BRANCH,
              W_BRANCH, W_BRANCH)
PROJ_TOTAL = sum(PROJ_SIZES)

kernel_name = "hybrid_gla_sgu_conv_s5_decode_step"

F32 = jnp.float32


def rmsnorm(x, g):
    xf = x.astype(F32)
    y = xf * lax.rsqrt(jnp.mean(xf * xf, axis=-1, keepdims=True) + EPS)
    return (y * g.astype(F32)).astype(x.dtype)


def gla_recurrence(q, k, v, log_a, s0):
    bsz, L = q.shape[0], q.shape[1]
    c = min(GLA_CHUNK, L)
    n = -(-L // c)
    pad = n * c - L

    def prep(t):
        t = jnp.pad(t.astype(F32), ((0, 0), (0, pad), (0, 0), (0, 0)))
        return t.reshape(bsz, n, c, t.shape[2], t.shape[3]).swapaxes(0, 1)

    qc, kc, vc, gc = prep(q), prep(k), prep(v), prep(log_a)
    causal = jnp.tril(jnp.ones((c, c), dtype=bool))[None, :, :, None, None]

    def step(S, inp):
        qb, kb, vb, gb = inp
        b = jnp.cumsum(gb, axis=1)
        o_inter = jnp.einsum('bthk,bhkv->bthv', qb * jnp.exp(b), S)
        diff = b[:, :, None] - b[:, None, :]
        decay = jnp.exp(jnp.where(causal, diff, -jnp.inf))
        scores = jnp.einsum('bthk,bshk,btshk->bths', qb, kb, decay)
        o_intra = jnp.einsum('bths,bshv->bthv', scores, vb)
        b_last = b[:, -1]
        S_new = jnp.exp(b_last)[..., None] * S + jnp.einsum(
            'bshk,bshv->bhkv', kb * jnp.exp(b_last[:, None] - b), vb)
        return S_new, o_inter + o_intra

    S_fin, o = lax.scan(step, s0.astype(F32), (qc, kc, vc, gc))
    o = o.swapaxes(0, 1).reshape(bsz, n * c, o.shape[3], o.shape[4])[:, :L]
    return o, S_fin


def spatial_gating(u, v_n, w_s, b_s):
    bsz, L = u.shape[0], u.shape[1]
    c = min(SGU_CHUNK, L)
    w = w_s[:, :c, :c] * jnp.tril(jnp.ones((c, c), dtype=w_s.dtype))
    vr = v_n.reshape(bsz, L // c, c, SGU_HEADS, SGU_HD)
    mixed = jnp.einsum('hts,bnshd->bnthd', w, vr) + b_s[:, :c].T[None, None, :, :, None]
    return u * mixed.reshape(bsz, L, W_BRANCH).astype(u.dtype)


def short_conv(z, buf, w):
    L = z.shape[1]
    zc = jnp.concatenate([buf.astype(z.dtype), z], axis=1)
    y = w[0] * zc[:, 0:L]
    for j in range(1, CONV_W):
        y = y + w[j] * zc[:, j:j + L]
    return y, zc[:, -(CONV_W - 1):]


def _cplx_combine(e1, e2):
    a1r, a1i, b1r, b1i = e1
    a2r, a2i, b2r, b2i = e2
    return (a2r * a1r - a2i * a1i,
            a2r * a1i + a2i * a1r,
            a2r * b1r - a2i * b1i + b2r,
            a2r * b1i + a2i * b1r + b2i)


def s5_scan(u, x0_re, x0_im, lam_re, lam_im, log_dt, b_re, b_im, c_re, c_im, d):
    bsz, L = u.shape[0], u.shape[1]
    uf = u.astype(F32).reshape(bsz, L, SSM_GROUPS, SSM_GROUP_SIZE)
    dt = jnp.exp(log_dt.astype(F32))[:, None]
    lr, li = lam_re.astype(F32), lam_im.astype(F32)
    mag = jnp.exp(lr * dt)
    ar, ai = mag * jnp.cos(li * dt), mag * jnp.sin(li * dt)
    den = lr * lr + li * li
    cr = ((ar - 1.0) * lr + ai * li) / den
    ci = (ai * lr - (ar - 1.0) * li) / den
    br, bi = b_re.astype(F32), b_im.astype(F32)
    bbr = cr[..., None] * br - ci[..., None] * bi
    bbi = cr[..., None] * bi + ci[..., None] * br
    in_r = jnp.einsum('gnp,blgp->blgn', bbr, uf)
    in_i = jnp.einsum('gnp,blgp->blgn', bbi, uf)
    x0r, x0i = x0_re.astype(F32), x0_im.astype(F32)
    in_r = in_r.at[:, 0].add(ar * x0r - ai * x0i)
    in_i = in_i.at[:, 0].add(ar * x0i + ai * x0r)
    a_r = jnp.broadcast_to(ar, in_r.shape)
    a_i = jnp.broadcast_to(ai, in_i.shape)
    _, _, xr, xi = lax.associative_scan(_cplx_combine, (a_r, a_i, in_r, in_i), axis=1)
    y = (jnp.einsum('gpn,blgn->blgp', c_re.astype(F32), xr)
         - jnp.einsum('gpn,blgn->blgp', c_im.astype(F32), xi)
         + d.astype(F32).reshape(SSM_GROUPS, SSM_GROUP_SIZE) * uf)
    return y.reshape(bsz, L, W_BRANCH).astype(u.dtype), xr[:, -1], xi[:, -1]


def mixer_layer(x, s_gla, conv_buf, ssm_re, ssm_im,
                norm_g, w_in, w_a2, b_a, gla_g, sgu_g, sgu_w, sgu_b, conv_w,
                lam_re, lam_im, log_dt, b_re, b_im, c_re, c_im, ssm_d, glu_w, glu_b, w_out):
    bsz, L, _ = x.shape
    dt = x.dtype
    h = rmsnorm(x, norm_g)
    p = h @ w_in
    offs = [int(o) for o in np.cumsum(PROJ_SIZES)[:-1]]
    (q, k, v, a_lr, g_a, u_b, v_b, g_b,
     cb_gate, cc_gate, h_c, g_c, u_d, g_d) = jnp.split(p, offs, axis=-1)

    q = q.reshape(bsz, L, GLA_HEADS, GLA_DK) * (GLA_DK ** -0.5)
    k = k.reshape(bsz, L, GLA_HEADS, GLA_DK)
    v = v.reshape(bsz, L, GLA_HEADS, GLA_DV)
    log_a = (jax.nn.log_sigmoid((a_lr @ w_a2 + b_a).astype(F32)) / GLA_TAU).reshape(bsz, L, GLA_HEADS, GLA_DK)
    o_a, s_gla_new = gla_recurrence(q, k, v, log_a, s_gla)
    o_a = o_a * lax.rsqrt(jnp.mean(o_a * o_a, axis=-1, keepdims=True) + EPS) * gla_g.astype(F32).reshape(GLA_HEADS, GLA_DV)
    o_a = o_a.reshape(bsz, L, W_BRANCH).astype(dt) * jax.nn.silu(g_a)

    vb = v_b.astype(F32).reshape(bsz, L, SGU_HEADS, SGU_HD)
    vb = vb - jnp.mean(vb, axis=-1, keepdims=True)
    v_n = vb * lax.rsqrt(jnp.mean(vb * vb, axis=-1, keepdims=True) + EPS) * sgu_g.astype(F32).reshape(SGU_HEADS, SGU_HD)
    v_n = v_n.reshape(bsz, L, W_BRANCH).astype(dt)
    o_b = spatial_gating(u_b, v_n, sgu_w, sgu_b) * jax.nn.silu(g_b)

    y_c, conv_new = short_conv(cc_gate * h_c, conv_buf, conv_w)
    o_c = cb_gate * y_c * jax.nn.silu(g_c)

    y_d, sr, si = s5_scan(u_d, ssm_re, ssm_im, lam_re, lam_im, log_dt, b_re, b_im, c_re, c_im, ssm_d)
    gd = jax.nn.gelu(y_d)
    o_d = gd * jax.nn.sigmoid(gd @ glu_w + glu_b) * jax.nn.silu(g_d)

    mixed = jnp.concatenate([o_a, o_b.astype(dt), o_c.astype(dt), o_d.astype(dt)], axis=-1)
    return x + mixed @ w_out, s_gla_new, conv_new, sr, si, v_n


def setup_inputs(seed: int = 0) -> dict:
    key = jax.random.key(seed)
    ks = jax.random.split(key, 32)
    nrm = jax.random.normal
    W = W_BRANCH
    lam_im = jnp.broadcast_to(math.pi * jnp.arange(SSM_N, dtype=F32), (DEPTH, SSM_GROUPS, SSM_N))
    return {
        "x_prompt": nrm(ks[0], (BATCH, SEQ, D_MODEL), F32),
        "x_sample": nrm(ks[1], (DEC_BATCH, DEC_SEQ, D_MODEL), F32),
        "state_gla": nrm(ks[2], (DEPTH, DEC_BATCH, GLA_HEADS, GLA_DK, GLA_DV), F32),
        "state_conv": nrm(ks[3], (DEPTH, DEC_BATCH, CONV_W - 1, W), F32),
        "state_ssm_re": 0.5 * nrm(ks[4], (DEPTH, DEC_BATCH, SSM_GROUPS, SSM_N), F32),
        "state_ssm_im": 0.5 * nrm(ks[5], (DEPTH, DEC_BATCH, SSM_GROUPS, SSM_N), F32),
        "norm_g": 1.0 + 0.01 * nrm(ks[6], (DEPTH, D_MODEL), F32),
        "w_in": nrm(ks[7], (DEPTH, D_MODEL, PROJ_TOTAL), F32) * D_MODEL ** -0.5,
        "w_a2": nrm(ks[8], (DEPTH, GLA_RANK, GLA_HEADS * GLA_DK), F32) * GLA_RANK ** -0.5,
        "b_a": 0.1 * nrm(ks[9], (DEPTH, GLA_HEADS * GLA_DK), F32),
        "gla_g": 1.0 + 0.01 * nrm(ks[10], (DEPTH, W), F32),
        "sgu_g": 1.0 + 0.01 * nrm(ks[11], (DEPTH, W), F32),
        "sgu_w": nrm(ks[12], (DEPTH, SGU_HEADS, SGU_CHUNK, SGU_CHUNK), F32) * SGU_CHUNK ** -0.5,
        "sgu_b": 1.0 + 0.01 * nrm(ks[13], (DEPTH, SGU_HEADS, SGU_CHUNK), F32),
        "conv_w": nrm(ks[14], (DEPTH, CONV_W, W), F32) * CONV_W ** -0.5,
        "ssm_lambda_re": -0.5 * jnp.exp(0.05 * nrm(ks[15], (DEPTH, SSM_GROUPS, SSM_N), F32)),
        "ssm_lambda_im": lam_im,
        "ssm_log_dt": jax.random.uniform(ks[16], (DEPTH, SSM_GROUPS), F32, math.log(1e-3), math.log(1e-1)),
        "ssm_b_re": nrm(ks[17], (DEPTH, SSM_GROUPS, SSM_N, SSM_GROUP_SIZE), F32) * (2.0 * SSM_GROUP_SIZE) ** -0.5,
        "ssm_b_im": nrm(ks[18], (DEPTH, SSM_GROUPS, SSM_N, SSM_GROUP_SIZE), F32) * (2.0 * SSM_GROUP_SIZE) ** -0.5,
        "ssm_c_re": nrm(ks[19], (DEPTH, SSM_GROUPS, SSM_GROUP_SIZE, SSM_N), F32) * (2.0 * SSM_N) ** -0.5,
        "ssm_c_im": nrm(ks[20], (DEPTH, SSM_GROUPS, SSM_GROUP_SIZE, SSM_N), F32) * (2.0 * SSM_N) ** -0.5,
        "ssm_d": nrm(ks[21], (DEPTH, W), F32),
        "glu_w": nrm(ks[22], (DEPTH, W, W), F32) * W ** -0.5,
        "glu_b": 0.01 * nrm(ks[23], (DEPTH, W), F32),
        "w_out": nrm(ks[24], (DEPTH, D_MIX, D_MODEL), F32) * D_MIX ** -0.5,
        "final_norm_g": 1.0 + 0.01 * nrm(ks[25], (D_MODEL,), F32),
    }


def reference(x_prompt, x_sample, state_gla, state_conv, state_ssm_re, state_ssm_im,
              norm_g, w_in, w_a2, b_a, gla_g, sgu_g, sgu_w, sgu_b, conv_w,
              ssm_lambda_re, ssm_lambda_im, ssm_log_dt, ssm_b_re, ssm_b_im, ssm_c_re, ssm_c_im,
              ssm_d, glu_w, glu_b, w_out, final_norm_g):
    bp = x_prompt.shape[0]
    hp, hs = x_prompt, x_sample
    gla_p, gla_s, conv_p, conv_s = [], [], [], []
    sre_p, sim_p, sre_s, sim_s, vn_s = [], [], [], [], []
    for l in range(DEPTH):
        lw = (norm_g[l], w_in[l], w_a2[l], b_a[l], gla_g[l], sgu_g[l], sgu_w[l], sgu_b[l], conv_w[l],
              ssm_lambda_re[l], ssm_lambda_im[l], ssm_log_dt[l], ssm_b_re[l], ssm_b_im[l],
              ssm_c_re[l], ssm_c_im[l], ssm_d[l], glu_w[l], glu_b[l], w_out[l])
        hp, g1, c1, r1, i1, _ = mixer_layer(
            hp, jnp.zeros((bp, GLA_HEADS, GLA_DK, GLA_DV), F32),
            jnp.zeros((bp, CONV_W - 1, W_BRANCH), hp.dtype),
            jnp.zeros((bp, SSM_GROUPS, SSM_N), F32), jnp.zeros((bp, SSM_GROUPS, SSM_N), F32), *lw)
        hs, g2, c2, r2, i2, v2 = mixer_layer(
            hs, state_gla[l], state_conv[l], state_ssm_re[l], state_ssm_im[l], *lw)
        gla_p.append(g1); gla_s.append(g2); conv_p.append(c1); conv_s.append(c2)
        sre_p.append(r1); sim_p.append(i1); sre_s.append(r2); sim_s.append(i2); vn_s.append(v2)
    y_prompt = rmsnorm(hp, final_norm_g)
    y_sample = rmsnorm(hs, final_norm_g)
    return (y_prompt, y_sample,
            jnp.stack(gla_p), jnp.stack(gla_s),
            jnp.stack(conv_p), jnp.stack(conv_s),
            jnp.stack(sre_p), jnp.stack(sim_p),
            jnp.stack(sre_s), jnp.stack(sim_s),
            jnp.stack(vn_s))
```

```python
import functools

import jax
import jax.numpy as jnp
from jax import lax
from jax.experimental import pallas as pl
from jax.experimental.pallas import tpu as pltpu

F32 = jnp.float32
BF16 = jnp.bfloat16
HIGHEST = lax.Precision.HIGHEST

D_MODEL = 2048
W_BRANCH = 512
EPS = 1e-6
GLA_HEADS = 4
GLA_DK = 64
GLA_DV = 128
GLA_RANK = 16
GLA_TAU = 16.0
SGU_HEADS = 4
SGU_HD = 128
SGU_CHUNK = 128
SSM_GROUPS = 32
SSM_GROUP_SIZE = 16
SSM_N = 64
SSM_STATE = SSM_GROUPS * SSM_N
PROJ_TOTAL = 6160

LANE = 128
SUBLANE = 8
PROJ_PAD = 6272
ALR_BLOCK = 6144 // LANE
COL_V, COL_GA, COL_UB, COL_VB, COL_GB, COL_CB, COL_CC, COL_HC, COL_GC, COL_UD, COL_GD = range(1, 12)
SSM_SLABS = W_BRANCH // LANE
SLAB_STATE = SSM_STATE // SSM_SLABS
GLA_CHUNK = 128
EXP_CLAMP = 80.0
VMEM_LIMIT = 56 * 1024 * 1024


def _cparams(sem):
    return pltpu.CompilerParams(dimension_semantics=sem, vmem_limit_bytes=VMEM_LIMIT)


def _silu(x):
    return x * jax.nn.sigmoid(x)


def _log_sigmoid(x):
    return jnp.minimum(x, 0.0) - jnp.log(1.0 + jnp.exp(-jnp.abs(x)))


def _dot(a, b):
    return jnp.dot(a, b, preferred_element_type=F32)


def _dot_exact(a, b):
    return jnp.dot(a, b, preferred_element_type=F32, precision=HIGHEST)


def _dot_nt(a, b):
    return lax.dot_general(a, b, (((1,), (1,)), ((), ())), preferred_element_type=F32)


def _inproj_kernel(x_ref, g_ref, w_ref, o_ref, h_ref):
    @pl.when(pl.program_id(1) == 0)
    def _():
        x = x_ref[...]
        ms = jnp.mean(x * x, axis=-1, keepdims=True)
        h_ref[...] = (x * lax.rsqrt(ms + EPS) * g_ref[...]).astype(BF16)

    o_ref[...] = _dot(h_ref[...], w_ref[...])


def _inproj(x, g, w, tm, tn):
    m = x.shape[0]
    return pl.pallas_call(
        _inproj_kernel,
        grid=(m // tm, PROJ_PAD // tn),
        in_specs=[pl.BlockSpec((tm, D_MODEL), lambda i, j: (i, 0)),
                  pl.BlockSpec((1, D_MODEL), lambda i, j: (0, 0)),
                  pl.BlockSpec((D_MODEL, tn), lambda i, j: (0, j))],
        out_specs=pl.BlockSpec((tm, tn), lambda i, j: (i, j)),
        out_shape=jax.ShapeDtypeStruct((m, PROJ_PAD), F32),
        scratch_shapes=[pltpu.VMEM((tm, D_MODEL), BF16)],
        compiler_params=_cparams(("parallel", "arbitrary")),
        name="inproj",
    )(x, g, w)


def _outproj_kernel(x_ref, oa_ref, ob_ref, oc_ref, od_ref, w_ref, g_ref, y_ref, *, final):
    acc = x_ref[...]
    for i, o_ref in enumerate((oa_ref, ob_ref, oc_ref, od_ref)):
        acc = acc + _dot(o_ref[...], w_ref[i * W_BRANCH:(i + 1) * W_BRANCH, :])
    if final:
        ms = jnp.mean(acc * acc, axis=-1, keepdims=True)
        acc = acc * lax.rsqrt(ms + EPS) * g_ref[...]
    y_ref[...] = acc


def _outproj(x, outs, w, g, tm, final):
    m = x.shape[0]
    row = lambda i: (i, 0)
    fixed = lambda i: (0, 0)
    return pl.pallas_call(
        functools.partial(_outproj_kernel, final=final),
        grid=(m // tm,),
        in_specs=[pl.BlockSpec((tm, D_MODEL), row)]
        + [pl.BlockSpec((tm, W_BRANCH), row)] * 4
        + [pl.BlockSpec((D_MODEL, D_MODEL), fixed), pl.BlockSpec((1, D_MODEL), fixed)],
        out_specs=pl.BlockSpec((tm, D_MODEL), row),
        out_shape=jax.ShapeDtypeStruct((m, D_MODEL), F32),
        compiler_params=_cparams(("parallel",)),
        name="outproj",
    )(x, *outs, w, g)


def _head_rmsnorm(o, gain, heads, width):
    parts = []
    for h in range(heads):
        oh = o[:, h * width:(h + 1) * width]
        ms = jnp.mean(oh * oh, axis=-1, keepdims=True)
        parts.append(oh * lax.rsqrt(ms + EPS))
    return jnp.concatenate(parts, axis=-1) * gain


def _sgu_layernorm(v, gain):
    parts = []
    for h in range(SGU_HEADS):
        vh = v[:, h * SGU_HD:(h + 1) * SGU_HD]
        vh = vh - jnp.mean(vh, axis=-1, keepdims=True)
        parts.append(vh * lax.rsqrt(jnp.mean(vh * vh, axis=-1, keepdims=True) + EPS))
    return jnp.concatenate(parts, axis=-1) * gain


def _gla_log_decay(alr, w_a2, b_a):
    z = _dot_exact(alr, w_a2) + b_a
    return _log_sigmoid(z) * (1.0 / GLA_TAU)


def _ssm_in(u, wb_ref):
    re, im = [], []
    for s in range(SSM_SLABS):
        r = _dot_exact(u[:, s * LANE:(s + 1) * LANE], wb_ref[s])
        re.append(r[:, :SLAB_STATE])
        im.append(r[:, SLAB_STATE:])
    return re, im


def _ssm_out(xr, xi, wc_ref, s):
    return _dot_exact(xr, wc_ref[s, :SLAB_STATE, :]) + _dot_exact(xi, wc_ref[s, SLAB_STATE:, :])


def _ssm_gate(y, u, gate, d, glu_w, glu_b):
    gd = jax.nn.gelu(y + d * u)
    return gd * jax.nn.sigmoid(_dot(gd.astype(BF16), glu_w) + glu_b) * _silu(gate)


def _ssm_prep_kernel(lr_ref, li_ref, ldt_ref, brt_ref, bit_ref, pwr_ref, pwi_ref, bbr_ref, bbi_ref):
    lr, li = lr_ref[...], li_ref[...]
    dt = jnp.exp(ldt_ref[...])
    for m in range(SUBLANE):
        mag = jnp.exp(lr * dt * (m + 1.0))
        pwr_ref[m] = mag * jnp.cos(li * dt * (m + 1.0))
        pwi_ref[m] = mag * jnp.sin(li * dt * (m + 1.0))
    ar, ai = pwr_ref[0], pwi_ref[0]
    den = lr * lr + li * li
    cr = ((ar - 1.0) * lr + ai * li) / den
    ci = (ai * lr - (ar - 1.0) * li) / den
    br, bi = brt_ref[...], bit_ref[...]
    bbr_ref[...] = cr * br - ci * bi
    bbi_ref[...] = cr * bi + ci * br


def _ssm_prep(lam_re, lam_im, log_dt, b_re_t, b_im_t):
    g, n, p = SSM_GROUPS, SSM_N, SSM_GROUP_SIZE
    return pl.pallas_call(
        _ssm_prep_kernel,
        out_shape=(jax.ShapeDtypeStruct((SUBLANE, g, 1, n), F32), jax.ShapeDtypeStruct((SUBLANE, g, 1, n), F32),
                   jax.ShapeDtypeStruct((g, p, n), F32), jax.ShapeDtypeStruct((g, p, n), F32)),
        name="ssm_prep",
    )(lam_re.reshape(g, 1, n), lam_im.reshape(g, 1, n), log_dt.reshape(g, 1, 1), b_re_t, b_im_t)


def _ssm_matrices(lam_re, lam_im, log_dt, b_re, b_im, c_re, c_im):
    pwr, pwi, bbr_t, bbi_t = _ssm_prep(lam_re, lam_im, log_dt,
                                       b_re.transpose(0, 2, 1), b_im.transpose(0, 2, 1))
    gs = SSM_GROUPS // SSM_SLABS
    eye = jnp.eye(gs, dtype=F32)

    def block_diag(blocks):
        s, _, r, c = blocks.shape
        return (blocks[:, :, :, None, :] * eye[None, :, None, :, None]).reshape(s, gs * r, gs * c)

    shape_b = (SSM_SLABS, gs, SSM_GROUP_SIZE, SSM_N)
    wb = jnp.concatenate([block_diag(bbr_t.reshape(shape_b)), block_diag(bbi_t.reshape(shape_b))], axis=-1)
    shape_c = (SSM_SLABS, gs, SSM_N, SSM_GROUP_SIZE)
    wc = jnp.concatenate([block_diag(c_re.transpose(0, 2, 1).reshape(shape_c)),
                          block_diag(-c_im.transpose(0, 2, 1).reshape(shape_c))], axis=1)
    return pwr.reshape(SUBLANE, SSM_STATE), pwi.reshape(SUBLANE, SSM_STATE), wb, wc


def _gla_kernel(q_ref, k_ref, v_ref, ga_ref, alr_ref, wa2_ref, ba_ref, gain_ref, o_ref, s_ref, st_ref):
    c = GLA_CHUNK
    i = pl.program_id(1)

    @pl.when(i == 0)
    def _():
        st_ref[...] = jnp.zeros_like(st_ref)

    q = q_ref[...] * (GLA_DK ** -0.5)
    k = k_ref[...]
    v = v_ref[...]
    la = _gla_log_decay(alr_ref[...], wa2_ref[...], ba_ref[...])
    row = lax.broadcasted_iota(jnp.int32, (c, c), 0)
    col = lax.broadcasted_iota(jnp.int32, (c, c), 1)
    b = _dot_exact((col <= row).astype(F32), la)
    b_last = b[c - 1:c, :]
    qe = q * jnp.exp(b)
    ke = k * jnp.exp(jnp.minimum(-b, EXP_CLAMP))
    kd = k * jnp.exp(b_last - b)

    key_head = lax.broadcasted_iota(jnp.int32, (c, GLA_HEADS * GLA_DK), 1) // GLA_DK
    val_head = lax.broadcasted_iota(jnp.int32, (c, W_BRANCH), 1) // GLA_DV
    k_bd = jnp.concatenate([jnp.where(key_head == h, ke, 0.0) for h in range(GLA_HEADS)], axis=0)
    v_bd = jnp.concatenate([jnp.where(val_head == h, v, 0.0) for h in range(GLA_HEADS)], axis=0)
    scores = _dot_nt(qe.astype(BF16), k_bd.astype(BF16))
    t_idx = lax.broadcasted_iota(jnp.int32, (c, GLA_HEADS * c), 0)
    s_idx = lax.broadcasted_iota(jnp.int32, (c, GLA_HEADS * c), 1) % c
    scores = jnp.where(s_idx <= t_idx, scores, 0.0)
    st = st_ref[...]
    o = _dot(scores.astype(BF16), v_bd.astype(BF16)) + _dot_nt(qe.astype(BF16), st.astype(BF16))

    upd = _dot(v.T.astype(BF16), kd.astype(BF16))
    st_row_head = lax.broadcasted_iota(jnp.int32, st.shape, 0) // GLA_DV
    st_col_head = lax.broadcasted_iota(jnp.int32, st.shape, 1) // GLA_DK
    st_new = st * jnp.exp(b_last) + jnp.where(st_row_head == st_col_head, upd, 0.0)
    st_ref[...] = st_new

    o = _head_rmsnorm(o, gain_ref[...], GLA_HEADS, GLA_DV)
    o_ref[...] = (o * _silu(ga_ref[...])).astype(o_ref.dtype)

    @pl.when(i == pl.num_programs(1) - 1)
    def _():
        for h in range(GLA_HEADS):
            slab_t = st_new[h * GLA_DV:(h + 1) * GLA_DV, :].T
            s_ref[h] = slab_t[h * GLA_DK:(h + 1) * GLA_DK, :]


def _gla_prompt(p3, w_a2p, b_a, gain):
    bsz, seq, _ = p3.shape
    c = GLA_CHUNK
    blk = lambda width, idx: pl.BlockSpec((None, c, width), lambda b, i: (b, i, idx))
    fixed = lambda shape: pl.BlockSpec(shape, lambda b, i: (0,) * len(shape))
    hk = GLA_HEADS * GLA_DK
    return pl.pallas_call(
        _gla_kernel,
        grid=(bsz, seq // c),
        in_specs=[blk(hk, 0), blk(hk, 1), blk(W_BRANCH, COL_V), blk(W_BRANCH, COL_GA), blk(LANE, ALR_BLOCK),
                  fixed((LANE, hk)), fixed((1, hk)), fixed((1, W_BRANCH))],
        out_specs=[pl.BlockSpec((None, c, W_BRANCH), lambda b, i: (b, i, 0)),
                   pl.BlockSpec((None, GLA_HEADS, GLA_DK, GLA_DV), lambda b, i: (b, 0, 0, 0))],
        out_shape=(jax.ShapeDtypeStruct((bsz, seq, W_BRANCH), BF16),
                   jax.ShapeDtypeStruct((bsz, GLA_HEADS, GLA_DK, GLA_DV), F32)),
        scratch_shapes=[pltpu.VMEM((GLA_HEADS * GLA_DV, hk), F32)],
        compiler_params=_cparams(("parallel", "arbitrary")),
        name="gla_prompt",
    )(p3, p3, p3, p3, p3, w_a2p, b_a, gain)


def _sgu_kernel(u_ref, v_ref, g_ref, gain_ref, w_ref, bt_ref, o_ref):
    c = SGU_CHUNK
    v_n = _sgu_layernorm(v_ref[...], gain_ref[...])
    row = lax.broadcasted_iota(jnp.int32, (c, c), 0)
    col = lax.broadcasted_iota(jnp.int32, (c, c), 1)
    parts = []
    for h in range(SGU_HEADS):
        w = jnp.where(col <= row, w_ref[h], 0.0)
        mixed = _dot(w.astype(BF16), v_n[:, h * SGU_HD:(h + 1) * SGU_HD].astype(BF16))
        parts.append(mixed + bt_ref[:, h:h + 1])
    mixed = jnp.concatenate(parts, axis=-1)
    o_ref[...] = (u_ref[...] * mixed * _silu(g_ref[...])).astype(o_ref.dtype)


def _sgu_prompt(p3, gain, sgu_w, sgu_b_t):
    bsz, seq, _ = p3.shape
    c = SGU_CHUNK
    blk = lambda idx: pl.BlockSpec((None, c, W_BRANCH), lambda b, i: (b, i, idx))
    fixed = lambda shape: pl.BlockSpec(shape, lambda b, i: (0,) * len(shape))
    return pl.pallas_call(
        _sgu_kernel,
        grid=(bsz, seq // c),
        in_specs=[blk(COL_UB), blk(COL_VB), blk(COL_GB), fixed((1, W_BRANCH)),
                  fixed((SGU_HEADS, c, c)), fixed((c, SGU_HEADS))],
        out_specs=pl.BlockSpec((None, c, W_BRANCH), lambda b, i: (b, i, 0)),
        out_shape=jax.ShapeDtypeStruct((bsz, seq, W_BRANCH), BF16),
        compiler_params=_cparams(("parallel", "parallel")),
        name="sgu_prompt",
    )(p3, p3, p3, gain, sgu_w, sgu_b_t)


def _conv_kernel(cb_ref, cc_ref, hc_ref, g_ref, w_ref, o_ref, new_ref, zbuf_ref):
    t = cb_ref.shape[0]
    i = pl.program_id(1)

    @pl.when(i == 0)
    def _():
        zbuf_ref[0:SUBLANE, :] = jnp.zeros((SUBLANE, W_BRANCH), F32)

    z = cc_ref[...] * hc_ref[...]
    zbuf_ref[SUBLANE:SUBLANE + t, :] = z
    w = w_ref[...]
    y = w[0:1, :] * zbuf_ref[SUBLANE - 2:SUBLANE - 2 + t, :]
    y = y + w[1:2, :] * zbuf_ref[SUBLANE - 1:SUBLANE - 1 + t, :]
    y = y + w[2:3, :] * z
    o_ref[...] = (cb_ref[...] * y * _silu(g_ref[...])).astype(o_ref.dtype)
    zbuf_ref[0:SUBLANE, :] = z[t - SUBLANE:t, :]

    @pl.when(i == pl.num_programs(1) - 1)
    def _():
        new_ref[:, 0:W_BRANCH] = z[t - 2:t - 1, :]
        new_ref[:, W_BRANCH:2 * W_BRANCH] = z[t - 1:t, :]


def _conv_prompt(p3, conv_w, tile):
    bsz, seq, _ = p3.shape
    blk = lambda idx: pl.BlockSpec((None, tile, W_BRANCH), lambda b, i: (b, i, idx))
    return pl.pallas_call(
        _conv_kernel,
        grid=(bsz, seq // tile),
        in_specs=[blk(COL_CB), blk(COL_CC), blk(COL_HC), blk(COL_GC),
                  pl.BlockSpec((3, W_BRANCH), lambda b, i: (0, 0))],
        out_specs=[pl.BlockSpec((None, tile, W_BRANCH), lambda b, i: (b, i, 0)),
                   pl.BlockSpec((None, 1, 2 * W_BRANCH), lambda b, i: (b, 0, 0))],
        out_shape=(jax.ShapeDtypeStruct((bsz, seq, W_BRANCH), BF16),
                   jax.ShapeDtypeStruct((bsz, 1, 2 * W_BRANCH), F32)),
        scratch_shapes=[pltpu.VMEM((tile + SUBLANE, W_BRANCH), F32)],
        compiler_params=_cparams(("parallel", "arbitrary")),
        name="conv_prompt",
    )(p3, p3, p3, p3, conv_w)


def _ssm_kernel(u_ref, g_ref, wb_ref, wc_ref, pwr_ref, pwi_ref, d_ref, gluw_ref, glub_ref,
                o_ref, sr_ref, si_ref, xr_ref, xi_ref, cr_ref, ci_ref, y_ref):
    t = u_ref.shape[0]
    i = pl.program_id(1)

    @pl.when(i == 0)
    def _():
        cr_ref[...] = jnp.zeros_like(cr_ref)
        ci_ref[...] = jnp.zeros_like(ci_ref)

    u = u_ref[...]
    re, im = _ssm_in(u, wb_ref)
    for s in range(SSM_SLABS):
        xr_ref[:, s * SLAB_STATE:(s + 1) * SLAB_STATE] = re[s]
        xi_ref[:, s * SLAB_STATE:(s + 1) * SLAB_STATE] = im[s]

    row = lax.broadcasted_iota(jnp.int32, (SUBLANE, SLAB_STATE), 0)
    for s in range(SSM_SLABS):
        lanes = slice(s * SLAB_STATE, (s + 1) * SLAB_STATE)
        pr, pi = pwr_ref[:, lanes], pwi_ref[:, lanes]
        steps = []
        for shift in (1, 2, 4):
            ar = jnp.where(row >= shift, pr[shift - 1:shift, :], 0.0)
            ai = jnp.where(row >= shift, pi[shift - 1:shift, :], 0.0)
            steps.append((shift, ar, ai))

        def tile_scan(j, carry, lanes=lanes, pr=pr, pi=pi, steps=steps):
            cr, ci = carry
            r0 = pl.multiple_of(j * SUBLANE, SUBLANE)
            xr = xr_ref[pl.ds(r0, SUBLANE), lanes]
            xi = xi_ref[pl.ds(r0, SUBLANE), lanes]
            for shift, ar, ai in steps:
                pxr = pltpu.roll(xr, shift, axis=0)
                pxi = pltpu.roll(xi, shift, axis=0)
                xr, xi = xr + ar * pxr - ai * pxi, xi + ar * pxi + ai * pxr
            xr, xi = xr + pr * cr - pi * ci, xi + pr * ci + pi * cr
            xr_ref[pl.ds(r0, SUBLANE), lanes] = xr
            xi_ref[pl.ds(r0, SUBLANE), lanes] = xi
            return xr[SUBLANE - 1:SUBLANE, :], xi[SUBLANE - 1:SUBLANE, :]

        cr, ci = lax.fori_loop(0, t // SUBLANE, tile_scan, (cr_ref[:, lanes], ci_ref[:, lanes]))
        cr_ref[:, lanes] = cr
        ci_ref[:, lanes] = ci
        y_ref[:, s * LANE:(s + 1) * LANE] = _ssm_out(xr_ref[:, lanes], xi_ref[:, lanes], wc_ref, s)

    o = _ssm_gate(y_ref[...], u, g_ref[...], d_ref[...], gluw_ref[...], glub_ref[...])
    o_ref[...] = o.astype(o_ref.dtype)

    @pl.when(i == pl.num_programs(1) - 1)
    def _():
        sr_ref[...] = cr_ref[...]
        si_ref[...] = ci_ref[...]


def _ssm_prompt(p3, wb, wc, pwr, pwi, d, glu_w, glu_b, tile):
    bsz, seq, _ = p3.shape
    blk = lambda idx: pl.BlockSpec((None, tile, W_BRANCH), lambda b, i: (b, i, idx))
    fixed = lambda shape: pl.BlockSpec(shape, lambda b, i: (0,) * len(shape))
    state = pl.BlockSpec((None, 1, SSM_STATE), lambda b, i: (b, 0, 0))
    return pl.pallas_call(
        _ssm_kernel,
        grid=(bsz, seq // tile),
        in_specs=[blk(COL_UD), blk(COL_GD), fixed(wb.shape), fixed(wc.shape), fixed(pwr.shape), fixed(pwi.shape),
                  fixed((1, W_BRANCH)), fixed((W_BRANCH, W_BRANCH)), fixed((1, W_BRANCH))],
        out_specs=[pl.BlockSpec((None, tile, W_BRANCH), lambda b, i: (b, i, 0)), state, state],
        out_shape=(jax.ShapeDtypeStruct((bsz, seq, W_BRANCH), BF16),
                   jax.ShapeDtypeStruct((bsz, 1, SSM_STATE), F32),
                   jax.ShapeDtypeStruct((bsz, 1, SSM_STATE), F32)),
        scratch_shapes=[pltpu.VMEM((tile, SSM_STATE), F32), pltpu.VMEM((tile, SSM_STATE), F32),
                        pltpu.VMEM((1, SSM_STATE), F32), pltpu.VMEM((1, SSM_STATE), F32),
                        pltpu.VMEM((tile, W_BRANCH), F32)],
        compiler_params=_cparams(("parallel", "arbitrary")),
        name="ssm_prompt",
    )(p3, p3, wb, wc, pwr, pwi, d, glu_w, glu_b)


def _decay_kernel(alr_ref, wa2_ref, ba_ref, a_ref):
    a_ref[...] = jnp.exp(_gla_log_decay(alr_ref[...], wa2_ref[...], ba_ref[...]))


def _sample_decay(p, w_a2p, b_a):
    n = p.shape[0]
    hk = GLA_HEADS * GLA_DK
    return pl.pallas_call(
        _decay_kernel,
        grid=(1,),
        in_specs=[pl.BlockSpec((n, LANE), lambda i: (0, ALR_BLOCK)),
                  pl.BlockSpec((LANE, hk), lambda i: (0, 0)), pl.BlockSpec((1, hk), lambda i: (0, 0))],
        out_specs=pl.BlockSpec((n, hk), lambda i: (0, 0)),
        out_shape=jax.ShapeDtypeStruct((n, hk), F32),
        name="sample_decay",
    )(p, w_a2p, b_a)


def _sample_kernel(at_ref, qt_ref, kt_ref, v_ref, ga_ref, ub_ref, vb_ref, gb_ref, cb_ref, cc_ref, hc_ref, gc_ref,
                   ud_ref, gd_ref, sgla_ref, sconv_ref, sre_ref, sim_ref,
                   glag_ref, sgug_ref, sguw_ref, sgub_ref, convw_ref,
                   wb_ref, wc_ref, pwr_ref, pwi_ref, d_ref, gluw_ref, glub_ref,
                   oa_ref, ob_ref, oc_ref, od_ref, ngla_ref, nconv_ref, nre_ref, nim_ref, vn_ref, orow_ref):
    rows = v_ref.shape[0]

    a_t = at_ref[...]
    q_t = qt_ref[...] * (GLA_DK ** -0.5)
    k_t = kt_ref[...]
    v = v_ref[...]
    for n in range(rows):
        for h in range(GLA_HEADS):
            keys = slice(h * GLA_DK, (h + 1) * GLA_DK)
            vals = slice(h * GLA_DV, (h + 1) * GLA_DV)
            s_new = a_t[keys, n:n + 1] * sgla_ref[n, h] + k_t[keys, n:n + 1] * v[n:n + 1, vals]
            ngla_ref[n, h] = s_new
            orow_ref[n:n + 1, vals] = jnp.sum(q_t[keys, n:n + 1] * s_new, axis=0, keepdims=True)
    o_a = _head_rmsnorm(orow_ref[...], glag_ref[...], GLA_HEADS, GLA_DV)
    oa_ref[...] = (o_a * _silu(ga_ref[...])).astype(oa_ref.dtype)

    v_n = _sgu_layernorm(vb_ref[...], sgug_ref[...])
    vn_ref[...] = v_n
    ob_ref[...] = (ub_ref[...] * (sguw_ref[...] * v_n + sgub_ref[...]) * _silu(gb_ref[...])).astype(ob_ref.dtype)

    z = cc_ref[...] * hc_ref[...]
    w = convw_ref[...]
    buf0, buf1 = sconv_ref[:, 0:W_BRANCH], sconv_ref[:, W_BRANCH:2 * W_BRANCH]
    y_c = w[0:1, :] * buf0 + w[1:2, :] * buf1 + w[2:3, :] * z
    oc_ref[...] = (cb_ref[...] * y_c * _silu(gc_ref[...])).astype(oc_ref.dtype)
    nconv_ref[:, 0:W_BRANCH] = buf1
    nconv_ref[:, W_BRANCH:2 * W_BRANCH] = z

    u = ud_ref[...]
    re, im = _ssm_in(u, wb_ref)
    ys = []
    for s in range(SSM_SLABS):
        lanes = slice(s * SLAB_STATE, (s + 1) * SLAB_STATE)
        ar, ai = pwr_ref[0:1, lanes], pwi_ref[0:1, lanes]
        x0r, x0i = sre_ref[:, lanes], sim_ref[:, lanes]
        xr = re[s] + (ar * x0r - ai * x0i)
        xi = im[s] + (ar * x0i + ai * x0r)
        nre_ref[:, lanes] = xr
        nim_ref[:, lanes] = xi
        ys.append(_ssm_out(xr, xi, wc_ref, s))
    y = jnp.concatenate(ys, axis=-1)
    od_ref[...] = _ssm_gate(y, u, gd_ref[...], d_ref[...], gluw_ref[...], glub_ref[...]).astype(od_ref.dtype)


def _sample_mixers(p, a, s_gla, s_conv, s_re, s_im, gla_g, sgu_g, sgu_w0, sgu_b0, conv_w,
                   wb, wc, pwr, pwi, d, glu_w, glu_b, rows):
    n = p.shape[0]
    hk = GLA_HEADS * GLA_DK
    to_cols = lambda m: m.reshape(n // rows, rows, hk).transpose(0, 2, 1)
    cols = pl.BlockSpec((None, hk, rows), lambda i: (i, 0, 0))
    col = lambda width, idx: pl.BlockSpec((rows, width), lambda i: (i, idx))
    fixed = lambda shape: pl.BlockSpec(shape, lambda i: (0,) * len(shape))
    wide = lambda width: pl.BlockSpec((rows, width), lambda i: (i, 0))
    gla_state = pl.BlockSpec((rows, GLA_HEADS, GLA_DK, GLA_DV), lambda i: (i, 0, 0, 0))
    p_cols = (COL_V, COL_GA, COL_UB, COL_VB, COL_GB, COL_CB, COL_CC, COL_HC, COL_GC, COL_UD, COL_GD)
    weights = (gla_g, sgu_g, sgu_w0, sgu_b0, conv_w, wb, wc, pwr, pwi, d, glu_w, glu_b)
    out_bf = jax.ShapeDtypeStruct((n, W_BRANCH), BF16)
    return pl.pallas_call(
        _sample_kernel,
        grid=(n // rows,),
        in_specs=[cols] * 3 + [col(W_BRANCH, idx) for idx in p_cols]
        + [gla_state, wide(2 * W_BRANCH), wide(SSM_STATE), wide(SSM_STATE)]
        + [fixed(w.shape) for w in weights],
        out_specs=[wide(W_BRANCH)] * 4 + [gla_state, wide(2 * W_BRANCH), wide(SSM_STATE), wide(SSM_STATE),
                                          wide(W_BRANCH)],
        out_shape=(out_bf, out_bf, out_bf, out_bf,
                   jax.ShapeDtypeStruct(s_gla.shape, F32), jax.ShapeDtypeStruct(s_conv.shape, F32),
                   jax.ShapeDtypeStruct(s_re.shape, F32), jax.ShapeDtypeStruct(s_im.shape, F32),
                   jax.ShapeDtypeStruct((n, W_BRANCH), F32)),
        scratch_shapes=[pltpu.VMEM((rows, W_BRANCH), F32)],
        compiler_params=_cparams(("parallel",)),
        name="sample_mixers",
    )(to_cols(a), to_cols(p[:, :hk]), to_cols(p[:, hk:2 * hk]), *([p] * len(p_cols)),
      s_gla, s_conv, s_re, s_im, *weights)


def _permute_w_in(w_in):
    a0 = 2 * GLA_HEADS * GLA_DK + W_BRANCH
    a1 = a0 + GLA_RANK
    pad = jnp.zeros((D_MODEL, PROJ_PAD - PROJ_TOTAL), w_in.dtype)
    return jnp.concatenate([w_in[:, :a0], w_in[:, a1:], w_in[:, a0:a1], pad], axis=1).astype(BF16)


def kernel(x_prompt, x_sample, state_gla, state_conv, state_ssm_re, state_ssm_im, norm_g, w_in, w_a2, b_a, gla_g,
           sgu_g, sgu_w, sgu_b, conv_w, ssm_lambda_re, ssm_lambda_im, ssm_log_dt, ssm_b_re, ssm_b_im, ssm_c_re,
           ssm_c_im, ssm_d, glu_w, glu_b, w_out, final_norm_g):
    bsz, seq, _ = x_prompt.shape
    nsamp, dec_seq, _ = x_sample.shape
    depth = w_in.shape[0]
    assert dec_seq == 1 and seq % GLA_CHUNK == 0 and seq % SGU_CHUNK == 0
    hp = x_prompt.reshape(bsz * seq, D_MODEL)
    hs = x_sample.reshape(nsamp, D_MODEL)
    final_g = final_norm_g.reshape(1, D_MODEL)
    hk = GLA_HEADS * GLA_DK
    outs = {name: [] for name in ("gla_p", "gla_s", "conv_p", "conv_s", "re_p", "im_p", "re_s", "im_s", "vn_s")}

    for l in range(depth):
        w_in_p = _permute_w_in(w_in[l])
        w_out_b = w_out[l].astype(BF16)
        glu_w_b = glu_w[l].astype(BF16)
        g_l = norm_g[l].reshape(1, D_MODEL)
        w_a2p = jnp.zeros((LANE, hk), F32).at[:GLA_RANK].set(w_a2[l])
        b_a_l = b_a[l].reshape(1, hk)
        gla_g_l = gla_g[l].reshape(1, W_BRANCH)
        sgu_g_l = sgu_g[l].reshape(1, W_BRANCH)
        d_l = ssm_d[l].reshape(1, W_BRANCH)
        glu_b_l = glu_b[l].reshape(1, W_BRANCH)
        pwr, pwi, wb, wc = _ssm_matrices(ssm_lambda_re[l], ssm_lambda_im[l], ssm_log_dt[l],
                                          ssm_b_re[l], ssm_b_im[l], ssm_c_re[l], ssm_c_im[l])
        final = l == depth - 1

        p3 = _inproj(hp, g_l, w_in_p, tm=512, tn=896).reshape(bsz, seq, PROJ_PAD)
        o_a, gla_new = _gla_prompt(p3, w_a2p, b_a_l, gla_g_l)
        o_b = _sgu_prompt(p3, sgu_g_l, sgu_w[l], sgu_b[l].T)
        o_c, conv_new = _conv_prompt(p3, conv_w[l], tile=512)
        o_d, re_new, im_new = _ssm_prompt(p3, wb, wc, pwr, pwi, d_l, glu_w_b, glu_b_l, tile=256)
        mixed = [o.reshape(bsz * seq, W_BRANCH) for o in (o_a, o_b, o_c, o_d)]
        hp = _outproj(hp, mixed, w_out_b, final_g, tm=256, final=final)
        outs["gla_p"].append(gla_new)
        outs["conv_p"].append(conv_new.reshape(bsz, 2, W_BRANCH))
        outs["re_p"].append(re_new.reshape(bsz, SSM_GROUPS, SSM_N))
        outs["im_p"].append(im_new.reshape(bsz, SSM_GROUPS, SSM_N))

        ps = _inproj(hs, g_l, w_in_p, tm=nsamp, tn=896)
        sgu_w0 = jnp.repeat(sgu_w[l][:, 0, 0], SGU_HD).reshape(1, W_BRANCH)
        sgu_b0 = jnp.repeat(sgu_b[l][:, 0], SGU_HD).reshape(1, W_BRANCH)
        (s_a, s_b, s_c, s_d, gla_s, conv_s, re_s, im_s, vn_s) = _sample_mixers(
            ps, _sample_decay(ps, w_a2p, b_a_l), state_gla[l], state_conv[l].reshape(nsamp, 2 * W_BRANCH),
            state_ssm_re[l].reshape(nsamp, SSM_STATE), state_ssm_im[l].reshape(nsamp, SSM_STATE),
            gla_g_l, sgu_g_l, sgu_w0, sgu_b0, conv_w[l], wb, wc, pwr, pwi, d_l, glu_w_b, glu_b_l,
            rows=16)
        hs = _outproj(hs, [s_a, s_b, s_c, s_d], w_out_b, final_g, tm=nsamp, final=final)
        outs["gla_s"].append(gla_s)
        outs["conv_s"].append(conv_s.reshape(nsamp, 2, W_BRANCH))
        outs["re_s"].append(re_s.reshape(nsamp, SSM_GROUPS, SSM_N))
        outs["im_s"].append(im_s.reshape(nsamp, SSM_GROUPS, SSM_N))
        outs["vn_s"].append(vn_s.reshape(nsamp, 1, W_BRANCH))

    stack = lambda name: jnp.stack(outs[name])
    return (hp.reshape(bsz, seq, D_MODEL), hs.reshape(nsamp, 1, D_MODEL),
            stack("gla_p"), stack("gla_s"), stack("conv_p"), stack("conv_s"),
            stack("re_p"), stack("im_p"), stack("re_s"), stack("im_s"), stack("vn_s"))
```

```python
import functools

import jax
import jax.numpy as jnp
from jax import lax
from jax.experimental import pallas as pl
from jax.experimental.pallas import tpu as pltpu

F32 = jnp.float32
BF16 = jnp.bfloat16
HIGHEST = lax.Precision.HIGHEST

D_MODEL = 2048
W_BRANCH = 512
EPS = 1e-6
GLA_HEADS = 4
GLA_DK = 64
GLA_DV = 128
GLA_HK = GLA_HEADS * GLA_DK
GLA_RANK = 16
GLA_TAU = 16.0
SGU_HEADS = 4
SGU_HD = 128
SGU_CHUNK = 128
SSM_GROUPS = 32
SSM_GROUP_SIZE = 16
SSM_N = 64
SSM_STATE = SSM_GROUPS * SSM_N
PROJ_TOTAL = 6160

LANE = 128
SUBLANE = 8
PROJ_PAD = 6272
ALR_SRC = 2 * GLA_HK + W_BRANCH
ALR_DST = PROJ_TOTAL - GLA_RANK
ALR_BLOCK = ALR_DST // LANE
COL_V, COL_GA, COL_UB, COL_VB, COL_GB, COL_CB, COL_CC, COL_HC, COL_GC, COL_UD, COL_GD = range(1, 12)
SSM_SLABS = W_BRANCH // LANE
SLAB_STATE = SSM_STATE // SSM_SLABS
GLA_CHUNK = 128
EXP_CLAMP = 80.0
VMEM_LIMIT = 56 * 1024 * 1024

INPROJ_ROWS, INPROJ_COLS = 1024, 896
OUTPROJ_ROWS = 512
WPREP_ROWS = 256
SGU_TILE = 512
CONV_TILE = 512
SSM_TILE = 256
SAMPLE_ROWS = 16


def _cparams(sem):
    return pltpu.CompilerParams(dimension_semantics=sem, vmem_limit_bytes=VMEM_LIMIT)


def _silu(x):
    return x * jax.nn.sigmoid(x)


def _log_sigmoid(x):
    return jnp.minimum(x, 0.0) - jnp.log(1.0 + jnp.exp(-jnp.abs(x)))


def _dot(a, b):
    return jnp.dot(a, b, preferred_element_type=F32)


def _dot_bf16(a, b):
    return jnp.dot(a.astype(BF16), b.astype(BF16), preferred_element_type=F32)


def _dot_exact(a, b):
    return jnp.dot(a, b, preferred_element_type=F32, precision=HIGHEST)


def _dot_nt(a, b):
    return lax.dot_general(a, b, (((1,), (1,)), ((), ())), preferred_element_type=F32)


def _layer(shape, l):
    return pl.BlockSpec((None,) + tuple(shape), lambda *_: (l,) + (0,) * len(shape))


def _drop_refs(fn, start, count):
    def wrapped(*refs):
        return fn(*refs[:start], *refs[start + count:])
    return wrapped


def _stacked_call(kernel_fn, n_in, prev, out_first, **kwargs):
    in_specs = list(kwargs.pop("in_specs"))
    aliases = {}
    if prev is not None:
        in_specs += [pl.BlockSpec(memory_space=pl.ANY)] * len(prev)
        aliases = {n_in + k: out_first + k for k in range(len(prev))}
        kernel_fn = _drop_refs(kernel_fn, n_in, len(prev))
    return pl.pallas_call(kernel_fn, in_specs=in_specs, input_output_aliases=aliases, **kwargs), \
        (() if prev is None else tuple(prev))


def _wprep_kernel(w_ref, o_ref):
    w = w_ref[...]
    o_ref[:, 0:ALR_SRC] = w[:, 0:ALR_SRC].astype(BF16)
    o_ref[:, ALR_SRC:ALR_DST] = w[:, ALR_SRC + GLA_RANK:PROJ_TOTAL].astype(BF16)
    tail = jnp.concatenate([w[:, ALR_SRC:ALR_SRC + GLA_RANK],
                            jnp.zeros((w.shape[0], PROJ_PAD - PROJ_TOTAL), F32)], axis=1)
    o_ref[:, ALR_DST:PROJ_PAD] = tail.astype(BF16)


def _prep_w_in(w_in):
    depth = w_in.shape[0]
    return pl.pallas_call(
        _wprep_kernel,
        grid=(depth, D_MODEL // WPREP_ROWS),
        in_specs=[pl.BlockSpec((None, WPREP_ROWS, PROJ_TOTAL), lambda l, i: (l, i, 0))],
        out_specs=pl.BlockSpec((None, WPREP_ROWS, PROJ_PAD), lambda l, i: (l, i, 0)),
        out_shape=jax.ShapeDtypeStruct((depth, D_MODEL, PROJ_PAD), BF16),
        compiler_params=_cparams(("parallel", "parallel")),
        name="prep_w_in",
    )(w_in)


def _inproj_kernel(x_ref, g_ref, w_ref, o_ref, h_ref):
    @pl.when(pl.program_id(1) == 0)
    def _():
        x = x_ref[...]
        ms = jnp.mean(x * x, axis=-1, keepdims=True)
        h_ref[...] = (x * lax.rsqrt(ms + EPS) * g_ref[...]).astype(BF16)

    o_ref[...] = _dot(h_ref[...], w_ref[...])


def _inproj(x, g, w, l, tm):
    m = x.shape[0]
    tn = INPROJ_COLS
    return pl.pallas_call(
        _inproj_kernel,
        grid=(m // tm, PROJ_PAD // tn),
        in_specs=[pl.BlockSpec((tm, D_MODEL), lambda i, j: (i, 0)),
                  _layer((1, D_MODEL), l),
                  pl.BlockSpec((None, D_MODEL, tn), lambda i, j: (l, 0, j))],
        out_specs=pl.BlockSpec((tm, tn), lambda i, j: (i, j)),
        out_shape=jax.ShapeDtypeStruct((m, PROJ_PAD), F32),
        scratch_shapes=[pltpu.VMEM((tm, D_MODEL), BF16)],
        compiler_params=_cparams(("parallel", "arbitrary")),
        name="inproj",
    )(x, g, w)


def _outproj_kernel(x_ref, oa_ref, ob_ref, oc_ref, od_ref, w_ref, g_ref, y_ref, *, final):
    acc = x_ref[...]
    for i, o_ref in enumerate((oa_ref, ob_ref, oc_ref, od_ref)):
        acc = acc + _dot(o_ref[...], w_ref[i * W_BRANCH:(i + 1) * W_BRANCH, :])
    if final:
        ms = jnp.mean(acc * acc, axis=-1, keepdims=True)
        acc = acc * lax.rsqrt(ms + EPS) * g_ref[...]
    y_ref[...] = acc


def _outproj(x, outs, w, g, l, tm, final):
    m = x.shape[0]
    row = lambda i: (i, 0)
    return pl.pallas_call(
        functools.partial(_outproj_kernel, final=final),
        grid=(m // tm,),
        in_specs=[pl.BlockSpec((tm, D_MODEL), row)]
        + [pl.BlockSpec((tm, W_BRANCH), row)] * 4
        + [_layer((D_MODEL, D_MODEL), l), pl.BlockSpec((1, D_MODEL), lambda i: (0, 0))],
        out_specs=pl.BlockSpec((tm, D_MODEL), row),
        out_shape=jax.ShapeDtypeStruct((m, D_MODEL), F32),
        compiler_params=_cparams(("parallel",)),
        name="outproj",
    )(x, *outs, w, g)


def _head_rmsnorm(o, gain, heads, width):
    parts = []
    for h in range(heads):
        oh = o[:, h * width:(h + 1) * width]
        ms = jnp.mean(oh * oh, axis=-1, keepdims=True)
        parts.append(oh * lax.rsqrt(ms + EPS))
    return jnp.concatenate(parts, axis=-1) * gain


def _sgu_layernorm(v, gain):
    parts = []
    for h in range(SGU_HEADS):
        vh = v[:, h * SGU_HD:(h + 1) * SGU_HD]
        vh = vh - jnp.mean(vh, axis=-1, keepdims=True)
        parts.append(vh * lax.rsqrt(jnp.mean(vh * vh, axis=-1, keepdims=True) + EPS))
    return jnp.concatenate(parts, axis=-1) * gain


def _gla_log_decay(alr, w_a2, b_a):
    z = _dot_exact(alr, w_a2) + b_a
    return _log_sigmoid(z) * (1.0 / GLA_TAU)


def _ssm_in(u, wb_ref, dot):
    re, im = [], []
    for s in range(SSM_SLABS):
        r = dot(u[:, s * LANE:(s + 1) * LANE], wb_ref[s])
        re.append(r[:, :SLAB_STATE])
        im.append(r[:, SLAB_STATE:])
    return re, im


def _ssm_out(xr, xi, wc_ref, s, dot):
    return dot(xr, wc_ref[s, :SLAB_STATE, :]) + dot(xi, wc_ref[s, SLAB_STATE:, :])


def _ssm_gate(y, u, gate, d, glu_w, glu_b):
    gd = jax.nn.gelu(y + d * u)
    return gd * jax.nn.sigmoid(_dot(gd.astype(BF16), glu_w) + glu_b) * _silu(gate)


def _ssm_prep_kernel(lr_ref, li_ref, ldt_ref, brt_ref, bit_ref, pwr_ref, pwi_ref, bbr_ref, bbi_ref):
    lr, li = lr_ref[...], li_ref[...]
    dt = jnp.exp(ldt_ref[...])
    for m in range(SUBLANE):
        mag = jnp.exp(lr * dt * (m + 1.0))
        pwr_ref[m] = mag * jnp.cos(li * dt * (m + 1.0))
        pwi_ref[m] = mag * jnp.sin(li * dt * (m + 1.0))
    ar, ai = pwr_ref[0], pwi_ref[0]
    den = lr * lr + li * li
    cr = ((ar - 1.0) * lr + ai * li) / den
    ci = (ai * lr - (ar - 1.0) * li) / den
    br, bi = brt_ref[...], bit_ref[...]
    bbr_ref[...] = cr * br - ci * bi
    bbi_ref[...] = cr * bi + ci * br


def _ssm_matrices(lam_re, lam_im, log_dt, b_re, b_im, c_re, c_im):
    depth = lam_re.shape[0]
    dg, n, p = depth * SSM_GROUPS, SSM_N, SSM_GROUP_SIZE
    pw_shape = jax.ShapeDtypeStruct((SUBLANE, dg, 1, n), F32)
    bb_shape = jax.ShapeDtypeStruct((dg, p, n), F32)
    pwr, pwi, bbr_t, bbi_t = pl.pallas_call(
        _ssm_prep_kernel, out_shape=(pw_shape, pw_shape, bb_shape, bb_shape), name="ssm_prep",
    )(lam_re.reshape(dg, 1, n), lam_im.reshape(dg, 1, n), log_dt.reshape(dg, 1, 1),
      b_re.transpose(0, 1, 3, 2).reshape(dg, p, n), b_im.transpose(0, 1, 3, 2).reshape(dg, p, n))
    gs = SSM_GROUPS // SSM_SLABS
    eye = jnp.eye(gs, dtype=F32)

    def block_diag(blocks):
        d, s, _, r, c = blocks.shape
        return (blocks[:, :, :, :, None, :] * eye[None, None, :, None, :, None]).reshape(d, s, gs * r, gs * c)

    shape_b = (depth, SSM_SLABS, gs, p, n)
    wb = jnp.concatenate([block_diag(bbr_t.reshape(shape_b)), block_diag(bbi_t.reshape(shape_b))], axis=-1)
    shape_c = (depth, SSM_SLABS, gs, n, p)
    wc = jnp.concatenate([block_diag(c_re.transpose(0, 1, 3, 2).reshape(shape_c)),
                          block_diag(-c_im.transpose(0, 1, 3, 2).reshape(shape_c))], axis=2)
    powers = lambda pw: pw.reshape(SUBLANE, depth, SSM_STATE).transpose(1, 0, 2)
    return powers(pwr), powers(pwi), wb, wc


def _gla_kernel(q_ref, k_ref, v_ref, ga_ref, alr_ref, wa2_ref, ba_ref, gain_ref, o_ref, s_ref, st_ref):
    c = GLA_CHUNK
    i = pl.program_id(1)

    @pl.when(i == 0)
    def _():
        st_ref[...] = jnp.zeros_like(st_ref)

    q = q_ref[...] * (GLA_DK ** -0.5)
    k = k_ref[...]
    v = v_ref[...]
    la = _gla_log_decay(alr_ref[...], wa2_ref[...], ba_ref[...])
    row = lax.broadcasted_iota(jnp.int32, (c, c), 0)
    col = lax.broadcasted_iota(jnp.int32, (c, c), 1)
    b = _dot_exact((col <= row).astype(F32), la)
    b_last = b[c - 1:c, :]
    qe = q * jnp.exp(b)
    ke = k * jnp.exp(jnp.minimum(-b, EXP_CLAMP))
    kd = k * jnp.exp(b_last - b)

    key_head = lax.broadcasted_iota(jnp.int32, (c, GLA_HK), 1) // GLA_DK
    val_head = lax.broadcasted_iota(jnp.int32, (c, W_BRANCH), 1) // GLA_DV
    k_bd = jnp.concatenate([jnp.where(key_head == h, ke, 0.0) for h in range(GLA_HEADS)], axis=0)
    v_bd = jnp.concatenate([jnp.where(val_head == h, v, 0.0) for h in range(GLA_HEADS)], axis=0)
    scores = _dot_nt(qe.astype(BF16), k_bd.astype(BF16))
    t_idx = lax.broadcasted_iota(jnp.int32, (c, GLA_HEADS * c), 0)
    s_idx = lax.broadcasted_iota(jnp.int32, (c, GLA_HEADS * c), 1) % c
    scores = jnp.where(s_idx <= t_idx, scores, 0.0)
    st = st_ref[...]
    o = _dot(scores.astype(BF16), v_bd.astype(BF16)) + _dot_nt(qe.astype(BF16), st.astype(BF16))

    upd = _dot(v.T.astype(BF16), kd.astype(BF16))
    st_row_head = lax.broadcasted_iota(jnp.int32, st.shape, 0) // GLA_DV
    st_col_head = lax.broadcasted_iota(jnp.int32, st.shape, 1) // GLA_DK
    st_new = st * jnp.exp(b_last) + jnp.where(st_row_head == st_col_head, upd, 0.0)
    st_ref[...] = st_new

    o = _head_rmsnorm(o, gain_ref[...], GLA_HEADS, GLA_DV)
    o_ref[...] = (o * _silu(ga_ref[...])).astype(o_ref.dtype)

    @pl.when(i == pl.num_programs(1) - 1)
    def _():
        for h in range(GLA_HEADS):
            slab_t = st_new[h * GLA_DV:(h + 1) * GLA_DV, :].T
            s_ref[h] = slab_t[h * GLA_DK:(h + 1) * GLA_DK, :]


def _gla_prompt(p3, w_a2p, b_a, gain, l, prev):
    bsz, seq, _ = p3.shape
    depth = w_a2p.shape[0]
    c = GLA_CHUNK
    blk = lambda width, idx: pl.BlockSpec((None, c, width), lambda b, i: (b, i, idx))
    call, extra = _stacked_call(
        _gla_kernel, 8, prev, 1,
        grid=(bsz, seq // c),
        in_specs=[blk(GLA_HK, 0), blk(GLA_HK, 1), blk(W_BRANCH, COL_V), blk(W_BRANCH, COL_GA), blk(LANE, ALR_BLOCK),
                  _layer((LANE, GLA_HK), l), _layer((1, GLA_HK), l), _layer((1, W_BRANCH), l)],
        out_specs=[pl.BlockSpec((None, c, W_BRANCH), lambda b, i: (b, i, 0)),
                   pl.BlockSpec((None, None, GLA_HEADS, GLA_DK, GLA_DV), lambda b, i: (l, b, 0, 0, 0))],
        out_shape=(jax.ShapeDtypeStruct((bsz, seq, W_BRANCH), BF16),
                   jax.ShapeDtypeStruct((depth, bsz, GLA_HEADS, GLA_DK, GLA_DV), F32)),
        scratch_shapes=[pltpu.VMEM((GLA_HEADS * GLA_DV, GLA_HK), F32)],
        compiler_params=_cparams(("parallel", "arbitrary")),
        name="gla_prompt")
    return call(p3, p3, p3, p3, p3, w_a2p, b_a, gain, *extra)


def _sgu_kernel(u_ref, v_ref, g_ref, gain_ref, w_ref, bt_ref, o_ref):
    c = SGU_CHUNK
    row = lax.broadcasted_iota(jnp.int32, (c, c), 0)
    col = lax.broadcasted_iota(jnp.int32, (c, c), 1)
    w = [jnp.where(col <= row, w_ref[h], 0.0).astype(BF16) for h in range(SGU_HEADS)]
    for r0 in range(0, u_ref.shape[0], c):
        rows = slice(r0, r0 + c)
        v_n = _sgu_layernorm(v_ref[rows, :], gain_ref[...])
        parts = []
        for h in range(SGU_HEADS):
            mixed = _dot(w[h], v_n[:, h * SGU_HD:(h + 1) * SGU_HD].astype(BF16))
            parts.append(mixed + bt_ref[:, h:h + 1])
        mixed = jnp.concatenate(parts, axis=-1)
        o_ref[rows, :] = (u_ref[rows, :] * mixed * _silu(g_ref[rows, :])).astype(o_ref.dtype)


def _sgu_prompt(p3, gain, sgu_w, sgu_b_t, l):
    bsz, seq, _ = p3.shape
    c = SGU_CHUNK
    tile = SGU_TILE
    blk = lambda idx: pl.BlockSpec((None, tile, W_BRANCH), lambda b, i: (b, i, idx))
    return pl.pallas_call(
        _sgu_kernel,
        grid=(bsz, seq // tile),
        in_specs=[blk(COL_UB), blk(COL_VB), blk(COL_GB), _layer((1, W_BRANCH), l),
                  _layer((SGU_HEADS, c, c), l), _layer((c, SGU_HEADS), l)],
        out_specs=pl.BlockSpec((None, tile, W_BRANCH), lambda b, i: (b, i, 0)),
        out_shape=jax.ShapeDtypeStruct((bsz, seq, W_BRANCH), BF16),
        compiler_params=_cparams(("parallel", "parallel")),
        name="sgu_prompt",
    )(p3, p3, p3, gain, sgu_w, sgu_b_t)


def _conv_kernel(cb_ref, cc_ref, hc_ref, g_ref, w_ref, o_ref, new_ref, zbuf_ref):
    t = cb_ref.shape[0]
    i = pl.program_id(1)

    @pl.when(i == 0)
    def _():
        zbuf_ref[0:SUBLANE, :] = jnp.zeros((SUBLANE, W_BRANCH), F32)

    z = cc_ref[...] * hc_ref[...]
    zbuf_ref[SUBLANE:SUBLANE + t, :] = z
    w = w_ref[...]
    y = w[0:1, :] * zbuf_ref[SUBLANE - 2:SUBLANE - 2 + t, :]
    y = y + w[1:2, :] * zbuf_ref[SUBLANE - 1:SUBLANE - 1 + t, :]
    y = y + w[2:3, :] * z
    o_ref[...] = (cb_ref[...] * y * _silu(g_ref[...])).astype(o_ref.dtype)
    zbuf_ref[0:SUBLANE, :] = z[t - SUBLANE:t, :]

    @pl.when(i == pl.num_programs(1) - 1)
    def _():
        new_ref[:, 0:W_BRANCH] = z[t - 2:t - 1, :]
        new_ref[:, W_BRANCH:2 * W_BRANCH] = z[t - 1:t, :]


def _conv_prompt(p3, conv_w, l, prev):
    bsz, seq, _ = p3.shape
    depth = conv_w.shape[0]
    tile = CONV_TILE
    blk = lambda idx: pl.BlockSpec((None, tile, W_BRANCH), lambda b, i: (b, i, idx))
    call, extra = _stacked_call(
        _conv_kernel, 5, prev, 1,
        grid=(bsz, seq // tile),
        in_specs=[blk(COL_CB), blk(COL_CC), blk(COL_HC), blk(COL_GC), _layer((3, W_BRANCH), l)],
        out_specs=[pl.BlockSpec((None, tile, W_BRANCH), lambda b, i: (b, i, 0)),
                   pl.BlockSpec((None, None, 1, 2 * W_BRANCH), lambda b, i: (l, b, 0, 0))],
        out_shape=(jax.ShapeDtypeStruct((bsz, seq, W_BRANCH), BF16),
                   jax.ShapeDtypeStruct((depth, bsz, 1, 2 * W_BRANCH), F32)),
        scratch_shapes=[pltpu.VMEM((tile + SUBLANE, W_BRANCH), F32)],
        compiler_params=_cparams(("parallel", "arbitrary")),
        name="conv_prompt")
    return call(p3, p3, p3, p3, conv_w, *extra)


def _ssm_kernel(u_ref, g_ref, wb_ref, wc_ref, pwr_ref, pwi_ref, d_ref, gluw_ref, glub_ref,
                o_ref, sr_ref, si_ref, xr_ref, xi_ref, cr_ref, ci_ref, y_ref):
    t = u_ref.shape[0]
    i = pl.program_id(1)

    @pl.when(i == 0)
    def _():
        cr_ref[...] = jnp.zeros_like(cr_ref)
        ci_ref[...] = jnp.zeros_like(ci_ref)

    u = u_ref[...]
    re, im = _ssm_in(u.astype(BF16), wb_ref, _dot)
    for s in range(SSM_SLABS):
        xr_ref[:, s * SLAB_STATE:(s + 1) * SLAB_STATE] = re[s]
        xi_ref[:, s * SLAB_STATE:(s + 1) * SLAB_STATE] = im[s]

    row = lax.broadcasted_iota(jnp.int32, (SUBLANE, SLAB_STATE), 0)
    for s in range(SSM_SLABS):
        lanes = slice(s * SLAB_STATE, (s + 1) * SLAB_STATE)
        pr, pi = pwr_ref[:, lanes], pwi_ref[:, lanes]
        steps = []
        for shift in (1, 2, 4):
            ar = jnp.where(row >= shift, pr[shift - 1:shift, :], 0.0)
            ai = jnp.where(row >= shift, pi[shift - 1:shift, :], 0.0)
            steps.append((shift, ar, ai))

        def tile_scan(j, carry, lanes=lanes, pr=pr, pi=pi, steps=steps):
            cr, ci = carry
            r0 = pl.multiple_of(j * SUBLANE, SUBLANE)
            xr = xr_ref[pl.ds(r0, SUBLANE), lanes]
            xi = xi_ref[pl.ds(r0, SUBLANE), lanes]
            for shift, ar, ai in steps:
                pxr = pltpu.roll(xr, shift, axis=0)
                pxi = pltpu.roll(xi, shift, axis=0)
                xr, xi = xr + ar * pxr - ai * pxi, xi + ar * pxi + ai * pxr
            xr, xi = xr + pr * cr - pi * ci, xi + pr * ci + pi * cr
            xr_ref[pl.ds(r0, SUBLANE), lanes] = xr
            xi_ref[pl.ds(r0, SUBLANE), lanes] = xi
            return xr[SUBLANE - 1:SUBLANE, :], xi[SUBLANE - 1:SUBLANE, :]

        cr, ci = lax.fori_loop(0, t // SUBLANE, tile_scan, (cr_ref[:, lanes], ci_ref[:, lanes]))
        cr_ref[:, lanes] = cr
        ci_ref[:, lanes] = ci
        y_ref[:, s * LANE:(s + 1) * LANE] = _ssm_out(xr_ref[:, lanes].astype(BF16), xi_ref[:, lanes].astype(BF16),
                                                     wc_ref, s, _dot)

    o = _ssm_gate(y_ref[...], u, g_ref[...], d_ref[...], gluw_ref[...], glub_ref[...])
    o_ref[...] = o.astype(o_ref.dtype)

    @pl.when(i == pl.num_programs(1) - 1)
    def _():
        sr_ref[...] = cr_ref[...]
        si_ref[...] = ci_ref[...]


def _ssm_prompt(p3, wb, wc, pwr, pwi, d, glu_w, glu_b, l, prev):
    bsz, seq, _ = p3.shape
    depth = wb.shape[0]
    tile = SSM_TILE
    blk = lambda idx: pl.BlockSpec((None, tile, W_BRANCH), lambda b, i: (b, i, idx))
    state = pl.BlockSpec((None, None, 1, SSM_STATE), lambda b, i: (l, b, 0, 0))
    state_shape = jax.ShapeDtypeStruct((depth, bsz, 1, SSM_STATE), F32)
    call, extra = _stacked_call(
        _ssm_kernel, 9, prev, 1,
        grid=(bsz, seq // tile),
        in_specs=[blk(COL_UD), blk(COL_GD), _layer(wb.shape[1:], l), _layer(wc.shape[1:], l),
                  _layer(pwr.shape[1:], l), _layer(pwi.shape[1:], l),
                  _layer((1, W_BRANCH), l), _layer((W_BRANCH, W_BRANCH), l), _layer((1, W_BRANCH), l)],
        out_specs=[pl.BlockSpec((None, tile, W_BRANCH), lambda b, i: (b, i, 0)), state, state],
        out_shape=(jax.ShapeDtypeStruct((bsz, seq, W_BRANCH), BF16), state_shape, state_shape),
        scratch_shapes=[pltpu.VMEM((tile, SSM_STATE), F32), pltpu.VMEM((tile, SSM_STATE), F32),
                        pltpu.VMEM((1, SSM_STATE), F32), pltpu.VMEM((1, SSM_STATE), F32),
                        pltpu.VMEM((tile, W_BRANCH), F32)],
        compiler_params=_cparams(("parallel", "arbitrary")),
        name="ssm_prompt")
    return call(p3, p3, wb, wc, pwr, pwi, d, glu_w, glu_b, *extra)


def _decay_kernel(alr_ref, wa2_ref, ba_ref, a_ref):
    a_ref[...] = jnp.exp(_gla_log_decay(alr_ref[...], wa2_ref[...], ba_ref[...]))


def _sample_decay(p, w_a2p, b_a, l):
    n = p.shape[0]
    return pl.pallas_call(
        _decay_kernel,
        grid=(1,),
        in_specs=[pl.BlockSpec((n, LANE), lambda i: (0, ALR_BLOCK)), _layer((LANE, GLA_HK), l), _layer((1, GLA_HK), l)],
        out_specs=pl.BlockSpec((n, GLA_HK), lambda i: (0, 0)),
        out_shape=jax.ShapeDtypeStruct((n, GLA_HK), F32),
        name="sample_decay",
    )(p, w_a2p, b_a)


def _sample_kernel(at_ref, qt_ref, kt_ref, v_ref, ga_ref, ub_ref, vb_ref, gb_ref, cb_ref, cc_ref, hc_ref, gc_ref,
                   ud_ref, gd_ref, sgla_ref, sconv_ref, sre_ref, sim_ref,
                   glag_ref, sgug_ref, sguw_ref, sgub_ref, convw_ref,
                   wb_ref, wc_ref, pwr_ref, pwi_ref, d_ref, gluw_ref, glub_ref,
                   oa_ref, ob_ref, oc_ref, od_ref, ngla_ref, nconv_ref, nre_ref, nim_ref, vn_ref, orow_ref):
    rows = v_ref.shape[0]

    a_t = at_ref[...]
    q_t = qt_ref[...] * (GLA_DK ** -0.5)
    k_t = kt_ref[...]
    v = v_ref[...]
    for n in range(rows):
        for h in range(GLA_HEADS):
            keys = slice(h * GLA_DK, (h + 1) * GLA_DK)
            vals = slice(h * GLA_DV, (h + 1) * GLA_DV)
            s_new = a_t[keys, n:n + 1] * sgla_ref[n, h] + k_t[keys, n:n + 1] * v[n:n + 1, vals]
            ngla_ref[n, h] = s_new
            orow_ref[n:n + 1, vals] = jnp.sum(q_t[keys, n:n + 1] * s_new, axis=0, keepdims=True)
    o_a = _head_rmsnorm(orow_ref[...], glag_ref[...], GLA_HEADS, GLA_DV)
    oa_ref[...] = (o_a * _silu(ga_ref[...])).astype(oa_ref.dtype)

    v_n = _sgu_layernorm(vb_ref[...], sgug_ref[...])
    vn_ref[...] = v_n
    ob_ref[...] = (ub_ref[...] * (sguw_ref[...] * v_n + sgub_ref[...]) * _silu(gb_ref[...])).astype(ob_ref.dtype)

    z = cc_ref[...] * hc_ref[...]
    w = convw_ref[...]
    buf0, buf1 = sconv_ref[:, 0:W_BRANCH], sconv_ref[:, W_BRANCH:2 * W_BRANCH]
    y_c = w[0:1, :] * buf0 + w[1:2, :] * buf1 + w[2:3, :] * z
    oc_ref[...] = (cb_ref[...] * y_c * _silu(gc_ref[...])).astype(oc_ref.dtype)
    nconv_ref[:, 0:W_BRANCH] = buf1
    nconv_ref[:, W_BRANCH:2 * W_BRANCH] = z

    u = ud_ref[...]
    re, im = _ssm_in(u, wb_ref, _dot_exact)
    ys = []
    for s in range(SSM_SLABS):
        lanes = slice(s * SLAB_STATE, (s + 1) * SLAB_STATE)
        ar, ai = pwr_ref[0:1, lanes], pwi_ref[0:1, lanes]
        x0r, x0i = sre_ref[:, lanes], sim_ref[:, lanes]
        xr = re[s] + (ar * x0r - ai * x0i)
        xi = im[s] + (ar * x0i + ai * x0r)
        nre_ref[:, lanes] = xr
        nim_ref[:, lanes] = xi
        ys.append(_ssm_out(xr, xi, wc_ref, s, _dot_exact))
    y = jnp.concatenate(ys, axis=-1)
    od_ref[...] = _ssm_gate(y, u, gd_ref[...], d_ref[...], gluw_ref[...], glub_ref[...]).astype(od_ref.dtype)


def _sample_mixers(p, a, states, weights, l, prev):
    n = p.shape[0]
    depth = states[0].shape[0]
    rows = SAMPLE_ROWS
    to_cols = lambda m: m.reshape(n // rows, rows, GLA_HK).transpose(0, 2, 1)
    cols = pl.BlockSpec((None, GLA_HK, rows), lambda i: (i, 0, 0))
    col = lambda idx: pl.BlockSpec((rows, W_BRANCH), lambda i: (i, idx))
    wide = lambda width: pl.BlockSpec((rows, width), lambda i: (i, 0))
    lwide = lambda width: pl.BlockSpec((None, rows, width), lambda i: (l, i, 0))
    gla_state = pl.BlockSpec((None, rows, GLA_HEADS, GLA_DK, GLA_DV), lambda i: (l, i, 0, 0, 0))
    state_specs = [gla_state, lwide(2 * W_BRANCH), lwide(SSM_STATE), lwide(SSM_STATE)]
    p_cols = (COL_V, COL_GA, COL_UB, COL_VB, COL_GB, COL_CB, COL_CC, COL_HC, COL_GC, COL_UD, COL_GD)
    out_bf = jax.ShapeDtypeStruct((n, W_BRANCH), BF16)
    n_in = 3 + len(p_cols) + len(states) + len(weights)
    call, extra = _stacked_call(
        _sample_kernel, n_in, prev, 4,
        grid=(n // rows,),
        in_specs=[cols] * 3 + [col(idx) for idx in p_cols] + state_specs
        + [_layer(w.shape[1:], l) for w in weights],
        out_specs=[wide(W_BRANCH)] * 4 + state_specs + [lwide(W_BRANCH)],
        out_shape=(out_bf, out_bf, out_bf, out_bf)
        + tuple(jax.ShapeDtypeStruct(s.shape, F32) for s in states)
        + (jax.ShapeDtypeStruct((depth, n, W_BRANCH), F32),),
        scratch_shapes=[pltpu.VMEM((rows, W_BRANCH), F32)],
        compiler_params=_cparams(("parallel",)),
        name="sample_mixers")
    return call(to_cols(a), to_cols(p[:, :GLA_HK]), to_cols(p[:, GLA_HK:2 * GLA_HK]), *([p] * len(p_cols)),
                *states, *weights, *extra)


def kernel(x_prompt, x_sample, state_gla, state_conv, state_ssm_re, state_ssm_im, norm_g, w_in, w_a2, b_a, gla_g,
           sgu_g, sgu_w, sgu_b, conv_w, ssm_lambda_re, ssm_lambda_im, ssm_log_dt, ssm_b_re, ssm_b_im, ssm_c_re,
           ssm_c_im, ssm_d, glu_w, glu_b, w_out, final_norm_g):
    bsz, seq, _ = x_prompt.shape
    nsamp, dec_seq, _ = x_sample.shape
    depth = w_in.shape[0]
    assert dec_seq == 1 and seq % GLA_CHUNK == 0 and seq % SGU_CHUNK == 0
    hp = x_prompt.reshape(bsz * seq, D_MODEL)
    hs = x_sample.reshape(nsamp, D_MODEL)

    vec = lambda a: a.reshape(depth, 1, a.shape[-1])
    w_in_p = _prep_w_in(w_in)
    w_out_b = w_out.astype(BF16)
    glu_w_b = glu_w.astype(BF16)
    norm_g3, b_a3, gla_g3, sgu_g3, d3, glu_b3 = map(vec, (norm_g, b_a, gla_g, sgu_g, ssm_d, glu_b))
    final_g = final_norm_g.reshape(1, D_MODEL)
    w_a2p = jnp.pad(w_a2, ((0, 0), (0, LANE - GLA_RANK), (0, 0)))
    sgu_b_t = sgu_b.transpose(0, 2, 1)
    sgu_w0 = vec(jnp.repeat(sgu_w[:, :, 0, 0], SGU_HD, axis=-1))
    sgu_b0 = vec(jnp.repeat(sgu_b[:, :, 0], SGU_HD, axis=-1))
    pwr, pwi, wb, wc = _ssm_matrices(ssm_lambda_re, ssm_lambda_im, ssm_log_dt, ssm_b_re, ssm_b_im,
                                     ssm_c_re, ssm_c_im)
    wb_b, wc_b = wb.astype(BF16), wc.astype(BF16)
    sample_states = (state_gla, state_conv.reshape(depth, nsamp, 2 * W_BRANCH),
                     state_ssm_re.reshape(depth, nsamp, SSM_STATE), state_ssm_im.reshape(depth, nsamp, SSM_STATE))
    sample_weights = (gla_g3, sgu_g3, sgu_w0, sgu_b0, conv_w, wb, wc, pwr, pwi, d3, glu_w_b, glu_b3)

    gla_p = conv_p = ssm_p = samp = None
    for l in range(depth):
        final = l == depth - 1

        p3 = _inproj(hp, norm_g3, w_in_p, l, INPROJ_ROWS).reshape(bsz, seq, PROJ_PAD)
        o_a, *gla_p = _gla_prompt(p3, w_a2p, b_a3, gla_g3, l, gla_p)
        o_b = _sgu_prompt(p3, sgu_g3, sgu_w, sgu_b_t, l)
        o_c, *conv_p = _conv_prompt(p3, conv_w, l, conv_p)
        o_d, *ssm_p = _ssm_prompt(p3, wb_b, wc_b, pwr, pwi, d3, glu_w_b, glu_b3, l, ssm_p)
        mixed = [o.reshape(bsz * seq, W_BRANCH) for o in (o_a, o_b, o_c, o_d)]
        hp = _outproj(hp, mixed, w_out_b, final_g, l, OUTPROJ_ROWS, final)

        ps = _inproj(hs, norm_g3, w_in_p, l, nsamp)
        s_a, s_b, s_c, s_d, *samp = _sample_mixers(ps, _sample_decay(ps, w_a2p, b_a3, l), sample_states,
                                                   sample_weights, l, samp)
        hs = _outproj(hs, [s_a, s_b, s_c, s_d], w_out_b, final_g, l, nsamp, final)

    gla_s, conv_s, re_s, im_s, vn_s = samp
    groups = lambda s: s.reshape(depth, -1, SSM_GROUPS, SSM_N)
    return (hp.reshape(bsz, seq, D_MODEL), hs.reshape(nsamp, 1, D_MODEL),
            gla_p[0], gla_s,
            conv_p[0].reshape(depth, bsz, 2, W_BRANCH), conv_s.reshape(depth, nsamp, 2, W_BRANCH),
            groups(ssm_p[0]), groups(ssm_p[1]), groups(re_s), groups(im_s),
            vn_s.reshape(depth, nsamp, 1, W_BRANCH))
```

```python
import functools

import jax
import jax.numpy as jnp
from jax import lax
from jax.experimental import pallas as pl
from jax.experimental.pallas import tpu as pltpu

F32 = jnp.float32
BF16 = jnp.bfloat16
HIGHEST = lax.Precision.HIGHEST

D_MODEL = 2048
W_BRANCH = 512
EPS = 1e-6
GLA_HEADS = 4
GLA_DK = 64
GLA_DV = 128
GLA_HK = GLA_HEADS * GLA_DK
GLA_RANK = 16
GLA_TAU = 16.0
SGU_HEADS = 4
SGU_HD = 128
SGU_CHUNK = 128
SSM_GROUPS = 32
SSM_GROUP_SIZE = 16
SSM_N = 64
SSM_STATE = SSM_GROUPS * SSM_N
PROJ_TOTAL = 6160

LANE = 128
SUBLANE = 8
MXU_TILE = 256
PROJ_PAD = 6400
ALR_SRC = 2 * GLA_HK + W_BRANCH
ALR_DST = PROJ_TOTAL - GLA_RANK
ALR_BLOCK = ALR_DST // LANE
COL_V, COL_GA, COL_UB, COL_VB, COL_GB, COL_CB, COL_CC, COL_HC, COL_GC, COL_UD, COL_GD = range(1, 12)
SSM_SLABS = W_BRANCH // LANE
SLAB_STATE = SSM_STATE // SSM_SLABS
GLA_CHUNK = 128
EXP_CLAMP = 80.0
VMEM_LIMIT = 56 * 1024 * 1024

INPROJ_ROWS, INPROJ_COLS = 1024, 5 * MXU_TILE
OUTPROJ_ROWS = 512
WPREP_ROWS = MXU_TILE
GLA_TILE = 512
SGU_TILE = 512
CONV_TILE = 512
SSM_TILE = 256
SAMPLE_ROWS = 16


def _cparams(sem):
    return pltpu.CompilerParams(dimension_semantics=sem, vmem_limit_bytes=VMEM_LIMIT)


def _silu(x):
    return x * jax.nn.sigmoid(x)


def _log_sigmoid(x):
    return jnp.minimum(x, 0.0) - jnp.log(1.0 + jnp.exp(-jnp.abs(x)))


def _dot(a, b):
    return jnp.dot(a, b, preferred_element_type=F32)


def _dot_bf16(a, b):
    return jnp.dot(a.astype(BF16), b.astype(BF16), preferred_element_type=F32)


def _dot_exact(a, b):
    return jnp.dot(a, b, preferred_element_type=F32, precision=HIGHEST)


def _dot_nt(a, b):
    return lax.dot_general(a, b, (((1,), (1,)), ((), ())), preferred_element_type=F32)


def _layer(shape, l):
    return pl.BlockSpec((None,) + tuple(shape), lambda *_: (l,) + (0,) * len(shape))


def _drop_refs(fn, start, count):
    def wrapped(*refs):
        return fn(*refs[:start], *refs[start + count:])
    return wrapped


def _stacked_call(kernel_fn, n_in, prev, out_first, **kwargs):
    in_specs = list(kwargs.pop("in_specs"))
    aliases = {}
    if prev is not None:
        in_specs += [pl.BlockSpec(memory_space=pl.ANY)] * len(prev)
        aliases = {n_in + k: out_first + k for k in range(len(prev))}
        kernel_fn = _drop_refs(kernel_fn, n_in, len(prev))
    return pl.pallas_call(kernel_fn, in_specs=in_specs, input_output_aliases=aliases, **kwargs), \
        (() if prev is None else tuple(prev))


def _wprep_kernel(w_ref, o_ref):
    last = pl.program_id(1) == pl.num_programs(1) - 1

    @pl.when(jnp.logical_not(last))
    def _():
        o_ref[...] = w_ref[0].astype(BF16)

    @pl.when(last)
    def _():
        o_ref[...] = jnp.zeros_like(o_ref)
        o_ref[0:GLA_RANK, :] = w_ref[0, 0:GLA_RANK, :].astype(BF16)


def _prep_w_in(w_in):
    depth = w_in.shape[0]
    w_t = jnp.swapaxes(w_in, 1, 2)
    tr = WPREP_ROWS
    n_head = ALR_SRC // tr
    n_body = ALR_DST // tr

    def src_row(j):
        u = GLA_RANK
        return jnp.where(j < n_head, j * (tr // u), jnp.where(j < n_body, j * (tr // u) + 1, ALR_SRC // u)) * u

    return pl.pallas_call(
        _wprep_kernel,
        grid=(depth, PROJ_PAD // tr),
        in_specs=[pl.BlockSpec((pl.Element(1), pl.Element(tr), pl.Element(D_MODEL)),
                               lambda l, j: (l, src_row(j), 0))],
        out_specs=pl.BlockSpec((None, tr, D_MODEL), lambda l, j: (l, j, 0)),
        out_shape=jax.ShapeDtypeStruct((depth, PROJ_PAD, D_MODEL), BF16),
        compiler_params=_cparams(("parallel", "parallel")),
        name="prep_w_in",
    )(w_t)


def _inproj_kernel(x_ref, g_ref, w_ref, o_ref, h_ref):
    @pl.when(pl.program_id(1) == 0)
    def _():
        x = x_ref[...]
        ms = jnp.mean(x * x, axis=-1, keepdims=True)
        h_ref[...] = (x * lax.rsqrt(ms + EPS) * g_ref[...]).astype(BF16)

    o_ref[...] = _dot_nt(h_ref[...], w_ref[...])


def _inproj(x, g, w, l, tm):
    m = x.shape[0]
    tn = INPROJ_COLS
    return pl.pallas_call(
        _inproj_kernel,
        grid=(m // tm, PROJ_PAD // tn),
        in_specs=[pl.BlockSpec((tm, D_MODEL), lambda i, j: (i, 0)),
                  _layer((1, D_MODEL), l),
                  pl.BlockSpec((None, tn, D_MODEL), lambda i, j: (l, j, 0))],
        out_specs=pl.BlockSpec((tm, tn), lambda i, j: (i, j)),
        out_shape=jax.ShapeDtypeStruct((m, PROJ_PAD), F32),
        scratch_shapes=[pltpu.VMEM((tm, D_MODEL), BF16)],
        compiler_params=_cparams(("parallel", "arbitrary")),
        name="inproj",
    )(x, g, w)


def _outproj_kernel(x_ref, oa_ref, ob_ref, oc_ref, od_ref, w_ref, g_ref, y_ref, *, final):
    acc = x_ref[...]
    for i, o_ref in enumerate((oa_ref, ob_ref, oc_ref, od_ref)):
        acc = acc + _dot(o_ref[...], w_ref[i * W_BRANCH:(i + 1) * W_BRANCH, :])
    if final:
        ms = jnp.mean(acc * acc, axis=-1, keepdims=True)
        acc = acc * lax.rsqrt(ms + EPS) * g_ref[...]
    y_ref[...] = acc


def _outproj(x, outs, w, g, l, tm, final):
    m = x.shape[0]
    row = lambda i: (i, 0)
    return pl.pallas_call(
        functools.partial(_outproj_kernel, final=final),
        grid=(m // tm,),
        in_specs=[pl.BlockSpec((tm, D_MODEL), row)]
        + [pl.BlockSpec((tm, W_BRANCH), row)] * 4
        + [_layer((D_MODEL, D_MODEL), l), pl.BlockSpec((1, D_MODEL), lambda i: (0, 0))],
        out_specs=pl.BlockSpec((tm, D_MODEL), row),
        out_shape=jax.ShapeDtypeStruct((m, D_MODEL), F32),
        compiler_params=_cparams(("parallel",)),
        name="outproj",
    )(x, *outs, w, g)


def _head_rmsnorm(o, gain, heads, width):
    parts = []
    for h in range(heads):
        oh = o[:, h * width:(h + 1) * width]
        ms = jnp.mean(oh * oh, axis=-1, keepdims=True)
        parts.append(oh * lax.rsqrt(ms + EPS))
    return jnp.concatenate(parts, axis=-1) * gain


def _sgu_layernorm(v, gain):
    parts = []
    for h in range(SGU_HEADS):
        vh = v[:, h * SGU_HD:(h + 1) * SGU_HD]
        vh = vh - jnp.mean(vh, axis=-1, keepdims=True)
        parts.append(vh * lax.rsqrt(jnp.mean(vh * vh, axis=-1, keepdims=True) + EPS))
    return jnp.concatenate(parts, axis=-1) * gain


def _gla_log_decay(alr, w_a2, b_a):
    z = _dot_exact(alr, w_a2) + b_a
    return _log_sigmoid(z) * (1.0 / GLA_TAU)


def _ssm_in(u, wb_ref, dot):
    re, im = [], []
    for s in range(SSM_SLABS):
        r = dot(u[:, s * LANE:(s + 1) * LANE], wb_ref[s])
        re.append(r[:, :SLAB_STATE])
        im.append(r[:, SLAB_STATE:])
    return re, im


def _ssm_out(xr, xi, wc_ref, s, dot):
    return dot(xr, wc_ref[s, :SLAB_STATE, :]) + dot(xi, wc_ref[s, SLAB_STATE:, :])


def _ssm_gate(y, u, gate, d, glu_w, glu_b):
    gd = jax.nn.gelu(y + d * u)
    return gd * jax.nn.sigmoid(_dot(gd.astype(BF16), glu_w) + glu_b) * _silu(gate)


def _ssm_prep_kernel(lr_ref, li_ref, ldt_ref, brt_ref, bit_ref, pwr_ref, pwi_ref, bbr_ref, bbi_ref):
    lr, li = lr_ref[...], li_ref[...]
    dt = jnp.exp(ldt_ref[...])
    for m in range(SUBLANE):
        mag = jnp.exp(lr * dt * (m + 1.0))
        pwr_ref[m] = mag * jnp.cos(li * dt * (m + 1.0))
        pwi_ref[m] = mag * jnp.sin(li * dt * (m + 1.0))
    ar, ai = pwr_ref[0], pwi_ref[0]
    den = lr * lr + li * li
    cr = ((ar - 1.0) * lr + ai * li) / den
    ci = (ai * lr - (ar - 1.0) * li) / den
    br, bi = brt_ref[...], bit_ref[...]
    bbr_ref[...] = cr * br - ci * bi
    bbi_ref[...] = cr * bi + ci * br


def _ssm_matrices(lam_re, lam_im, log_dt, b_re, b_im, c_re, c_im):
    depth = lam_re.shape[0]
    dg, n, p = depth * SSM_GROUPS, SSM_N, SSM_GROUP_SIZE
    pw_shape = jax.ShapeDtypeStruct((SUBLANE, dg, 1, n), F32)
    bb_shape = jax.ShapeDtypeStruct((dg, p, n), F32)
    pwr, pwi, bbr_t, bbi_t = pl.pallas_call(
        _ssm_prep_kernel, out_shape=(pw_shape, pw_shape, bb_shape, bb_shape), name="ssm_prep",
    )(lam_re.reshape(dg, 1, n), lam_im.reshape(dg, 1, n), log_dt.reshape(dg, 1, 1),
      b_re.transpose(0, 1, 3, 2).reshape(dg, p, n), b_im.transpose(0, 1, 3, 2).reshape(dg, p, n))
    gs = SSM_GROUPS // SSM_SLABS
    eye = jnp.eye(gs, dtype=F32)

    def block_diag(blocks):
        d, s, _, r, c = blocks.shape
        return (blocks[:, :, :, :, None, :] * eye[None, None, :, None, :, None]).reshape(d, s, gs * r, gs * c)

    shape_b = (depth, SSM_SLABS, gs, p, n)
    wb = jnp.concatenate([block_diag(bbr_t.reshape(shape_b)), block_diag(bbi_t.reshape(shape_b))], axis=-1)
    shape_c = (depth, SSM_SLABS, gs, n, p)
    wc = jnp.concatenate([block_diag(c_re.transpose(0, 1, 3, 2).reshape(shape_c)),
                          block_diag(-c_im.transpose(0, 1, 3, 2).reshape(shape_c))], axis=2)
    powers = lambda pw: pw.reshape(SUBLANE, depth, SSM_STATE).transpose(1, 0, 2)
    return powers(pwr), powers(pwi), wb, wc


def _gla_kernel(q_ref, k_ref, v_ref, ga_ref, alr_ref, wa2_ref, ba_ref, gain_ref, o_ref, s_ref, st_ref):
    c = GLA_CHUNK
    i = pl.program_id(1)

    @pl.when(i == 0)
    def _():
        st_ref[...] = jnp.zeros_like(st_ref)

    row = lax.broadcasted_iota(jnp.int32, (c, c), 0)
    col = lax.broadcasted_iota(jnp.int32, (c, c), 1)
    tril = (col <= row).astype(F32)
    key_head = lax.broadcasted_iota(jnp.int32, (c, GLA_HK), 1) // GLA_DK
    val_head = lax.broadcasted_iota(jnp.int32, (c, W_BRANCH), 1) // GLA_DV
    t_idx = lax.broadcasted_iota(jnp.int32, (c, GLA_HEADS * c), 0)
    s_idx = lax.broadcasted_iota(jnp.int32, (c, GLA_HEADS * c), 1) % c
    st_row_head = lax.broadcasted_iota(jnp.int32, st_ref.shape, 0) // GLA_DV
    st_col_head = lax.broadcasted_iota(jnp.int32, st_ref.shape, 1) // GLA_DK
    same_head = st_row_head == st_col_head

    st = st_ref[...]
    for r0 in range(0, q_ref.shape[0], c):
        rows = slice(r0, r0 + c)
        q = q_ref[rows, :] * (GLA_DK ** -0.5)
        k = k_ref[rows, :]
        v = v_ref[rows, :]
        la = _gla_log_decay(alr_ref[rows, :], wa2_ref[...], ba_ref[...])
        b = _dot_exact(tril, la)
        b_last = b[c - 1:c, :]
        qe = (q * jnp.exp(b)).astype(BF16)
        ke = k * jnp.exp(jnp.minimum(-b, EXP_CLAMP))
        kd = k * jnp.exp(b_last - b)
        k_bd = jnp.concatenate([jnp.where(key_head == h, ke, 0.0) for h in range(GLA_HEADS)], axis=0)
        v_bd = jnp.concatenate([jnp.where(val_head == h, v, 0.0) for h in range(GLA_HEADS)], axis=0)
        scores = _dot_nt(qe, k_bd.astype(BF16))
        scores = jnp.where(s_idx <= t_idx, scores, 0.0)
        o = _dot(scores.astype(BF16), v_bd.astype(BF16)) + _dot_nt(qe, st.astype(BF16))
        upd = _dot(v.T.astype(BF16), kd.astype(BF16))
        st = st * jnp.exp(b_last) + jnp.where(same_head, upd, 0.0)
        o = _head_rmsnorm(o, gain_ref[...], GLA_HEADS, GLA_DV)
        o_ref[rows, :] = (o * _silu(ga_ref[rows, :])).astype(o_ref.dtype)
    st_ref[...] = st

    @pl.when(i == pl.num_programs(1) - 1)
    def _():
        for h in range(GLA_HEADS):
            slab_t = st[h * GLA_DV:(h + 1) * GLA_DV, :].T
            s_ref[h] = slab_t[h * GLA_DK:(h + 1) * GLA_DK, :]


def _gla_prompt(p3, w_a2p, b_a, gain, l, prev):
    bsz, seq, _ = p3.shape
    depth = w_a2p.shape[0]
    c = GLA_TILE
    blk = lambda width, idx: pl.BlockSpec((None, c, width), lambda b, i: (b, i, idx))
    call, extra = _stacked_call(
        _gla_kernel, 8, prev, 1,
        grid=(bsz, seq // c),
        in_specs=[blk(GLA_HK, 0), blk(GLA_HK, 1), blk(W_BRANCH, COL_V), blk(W_BRANCH, COL_GA), blk(LANE, ALR_BLOCK),
                  _layer((LANE, GLA_HK), l), _layer((1, GLA_HK), l), _layer((1, W_BRANCH), l)],
        out_specs=[pl.BlockSpec((None, c, W_BRANCH), lambda b, i: (b, i, 0)),
                   pl.BlockSpec((None, None, GLA_HEADS, GLA_DK, GLA_DV), lambda b, i: (l, b, 0, 0, 0))],
        out_shape=(jax.ShapeDtypeStruct((bsz, seq, W_BRANCH), BF16),
                   jax.ShapeDtypeStruct((depth, bsz, GLA_HEADS, GLA_DK, GLA_DV), F32)),
        scratch_shapes=[pltpu.VMEM((GLA_HEADS * GLA_DV, GLA_HK), F32)],
        compiler_params=_cparams(("parallel", "arbitrary")),
        name="gla_prompt")
    return call(p3, p3, p3, p3, p3, w_a2p, b_a, gain, *extra)


def _sgu_kernel(u_ref, v_ref, g_ref, gain_ref, w_ref, bt_ref, o_ref):
    c = SGU_CHUNK
    row = lax.broadcasted_iota(jnp.int32, (c, c), 0)
    col = lax.broadcasted_iota(jnp.int32, (c, c), 1)
    w = [jnp.where(col <= row, w_ref[h], 0.0).astype(BF16) for h in range(SGU_HEADS)]
    for r0 in range(0, u_ref.shape[0], c):
        rows = slice(r0, r0 + c)
        v_n = _sgu_layernorm(v_ref[rows, :], gain_ref[...])
        parts = []
        for h in range(SGU_HEADS):
            mixed = _dot(w[h], v_n[:, h * SGU_HD:(h + 1) * SGU_HD].astype(BF16))
            parts.append(mixed + bt_ref[:, h:h + 1])
        mixed = jnp.concatenate(parts, axis=-1)
        o_ref[rows, :] = (u_ref[rows, :] * mixed * _silu(g_ref[rows, :])).astype(o_ref.dtype)


def _sgu_prompt(p3, gain, sgu_w, sgu_b_t, l):
    bsz, seq, _ = p3.shape
    c = SGU_CHUNK
    tile = SGU_TILE
    blk = lambda idx: pl.BlockSpec((None, tile, W_BRANCH), lambda b, i: (b, i, idx))
    return pl.pallas_call(
        _sgu_kernel,
        grid=(bsz, seq // tile),
        in_specs=[blk(COL_UB), blk(COL_VB), blk(COL_GB), _layer((1, W_BRANCH), l),
                  _layer((SGU_HEADS, c, c), l), _layer((c, SGU_HEADS), l)],
        out_specs=pl.BlockSpec((None, tile, W_BRANCH), lambda b, i: (b, i, 0)),
        out_shape=jax.ShapeDtypeStruct((bsz, seq, W_BRANCH), BF16),
        compiler_params=_cparams(("parallel", "parallel")),
        name="sgu_prompt",
    )(p3, p3, p3, gain, sgu_w, sgu_b_t)


def _conv_kernel(cb_ref, cc_ref, hc_ref, g_ref, w_ref, o_ref, new_ref, zbuf_ref):
    t = cb_ref.shape[0]
    i = pl.program_id(1)

    @pl.when(i == 0)
    def _():
        zbuf_ref[0:SUBLANE, :] = jnp.zeros((SUBLANE, W_BRANCH), F32)

    z = cc_ref[...] * hc_ref[...]
    zbuf_ref[SUBLANE:SUBLANE + t, :] = z
    w = w_ref[...]
    y = w[0:1, :] * zbuf_ref[SUBLANE - 2:SUBLANE - 2 + t, :]
    y = y + w[1:2, :] * zbuf_ref[SUBLANE - 1:SUBLANE - 1 + t, :]
    y = y + w[2:3, :] * z
    o_ref[...] = (cb_ref[...] * y * _silu(g_ref[...])).astype(o_ref.dtype)
    zbuf_ref[0:SUBLANE, :] = z[t - SUBLANE:t, :]

    @pl.when(i == pl.num_programs(1) - 1)
    def _():
        new_ref[:, 0:W_BRANCH] = z[t - 2:t - 1, :]
        new_ref[:, W_BRANCH:2 * W_BRANCH] = z[t - 1:t, :]


def _conv_prompt(p3, conv_w, l, prev):
    bsz, seq, _ = p3.shape
    depth = conv_w.shape[0]
    tile = CONV_TILE
    blk = lambda idx: pl.BlockSpec((None, tile, W_BRANCH), lambda b, i: (b, i, idx))
    call, extra = _stacked_call(
        _conv_kernel, 5, prev, 1,
        grid=(bsz, seq // tile),
        in_specs=[blk(COL_CB), blk(COL_CC), blk(COL_HC), blk(COL_GC), _layer((3, W_BRANCH), l)],
        out_specs=[pl.BlockSpec((None, tile, W_BRANCH), lambda b, i: (b, i, 0)),
                   pl.BlockSpec((None, None, 1, 2 * W_BRANCH), lambda b, i: (l, b, 0, 0))],
        out_shape=(jax.ShapeDtypeStruct((bsz, seq, W_BRANCH), BF16),
                   jax.ShapeDtypeStruct((depth, bsz, 1, 2 * W_BRANCH), F32)),
        scratch_shapes=[pltpu.VMEM((tile + SUBLANE, W_BRANCH), F32)],
        compiler_params=_cparams(("parallel", "arbitrary")),
        name="conv_prompt")
    return call(p3, p3, p3, p3, conv_w, *extra)


def _ssm_kernel(u_ref, g_ref, wb_ref, wc_ref, pwr_ref, pwi_ref, d_ref, gluw_ref, glub_ref,
                o_ref, sr_ref, si_ref, xr_ref, xi_ref, cr_ref, ci_ref, y_ref):
    t = u_ref.shape[0]
    i = pl.program_id(1)

    @pl.when(i == 0)
    def _():
        cr_ref[...] = jnp.zeros_like(cr_ref)
        ci_ref[...] = jnp.zeros_like(ci_ref)

    u = u_ref[...]
    re, im = _ssm_in(u.astype(BF16), wb_ref, _dot)
    for s in range(SSM_SLABS):
        xr_ref[:, s * SLAB_STATE:(s + 1) * SLAB_STATE] = re[s]
        xi_ref[:, s * SLAB_STATE:(s + 1) * SLAB_STATE] = im[s]

    row = lax.broadcasted_iota(jnp.int32, (SUBLANE, SLAB_STATE), 0)
    for s in range(SSM_SLABS):
        lanes = slice(s * SLAB_STATE, (s + 1) * SLAB_STATE)
        pr, pi = pwr_ref[:, lanes], pwi_ref[:, lanes]
        steps = []
        for shift in (1, 2, 4):
            ar = jnp.where(row >= shift, pr[shift - 1:shift, :], 0.0)
            ai = jnp.where(row >= shift, pi[shift - 1:shift, :], 0.0)
            steps.append((shift, ar, ai))

        def tile_scan(j, carry, lanes=lanes, pr=pr, pi=pi, steps=steps):
            cr, ci = carry
            r0 = pl.multiple_of(j * SUBLANE, SUBLANE)
            xr = xr_ref[pl.ds(r0, SUBLANE), lanes]
            xi = xi_ref[pl.ds(r0, SUBLANE), lanes]
            for shift, ar, ai in steps:
                pxr = pltpu.roll(xr, shift, axis=0)
                pxi = pltpu.roll(xi, shift, axis=0)
                xr, xi = xr + ar * pxr - ai * pxi, xi + ar * pxi + ai * pxr
            xr, xi = xr + pr * cr - pi * ci, xi + pr * ci + pi * cr
            xr_ref[pl.ds(r0, SUBLANE), lanes] = xr
            xi_ref[pl.ds(r0, SUBLANE), lanes] = xi
            return xr[SUBLANE - 1:SUBLANE, :], xi[SUBLANE - 1:SUBLANE, :]

        cr, ci = lax.fori_loop(0, t // SUBLANE, tile_scan, (cr_ref[:, lanes], ci_ref[:, lanes]))
        cr_ref[:, lanes] = cr
        ci_ref[:, lanes] = ci
        y_ref[:, s * LANE:(s + 1) * LANE] = _ssm_out(xr_ref[:, lanes].astype(BF16), xi_ref[:, lanes].astype(BF16),
                                                     wc_ref, s, _dot)

    o = _ssm_gate(y_ref[...], u, g_ref[...], d_ref[...], gluw_ref[...], glub_ref[...])
    o_ref[...] = o.astype(o_ref.dtype)

    @pl.when(i == pl.num_programs(1) - 1)
    def _():
        sr_ref[...] = cr_ref[...]
        si_ref[...] = ci_ref[...]


def _ssm_prompt(p3, wb, wc, pwr, pwi, d, glu_w, glu_b, l, prev):
    bsz, seq, _ = p3.shape
    depth = wb.shape[0]
    tile = SSM_TILE
    blk = lambda idx: pl.BlockSpec((None, tile, W_BRANCH), lambda b, i: (b, i, idx))
    state = pl.BlockSpec((None, None, 1, SSM_STATE), lambda b, i: (l, b, 0, 0))
    state_shape = jax.ShapeDtypeStruct((depth, bsz, 1, SSM_STATE), F32)
    call, extra = _stacked_call(
        _ssm_kernel, 9, prev, 1,
        grid=(bsz, seq // tile),
        in_specs=[blk(COL_UD), blk(COL_GD), _layer(wb.shape[1:], l), _layer(wc.shape[1:], l),
                  _layer(pwr.shape[1:], l), _layer(pwi.shape[1:], l),
                  _layer((1, W_BRANCH), l), _layer((W_BRANCH, W_BRANCH), l), _layer((1, W_BRANCH), l)],
        out_specs=[pl.BlockSpec((None, tile, W_BRANCH), lambda b, i: (b, i, 0)), state, state],
        out_shape=(jax.ShapeDtypeStruct((bsz, seq, W_BRANCH), BF16), state_shape, state_shape),
        scratch_shapes=[pltpu.VMEM((tile, SSM_STATE), F32), pltpu.VMEM((tile, SSM_STATE), F32),
                        pltpu.VMEM((1, SSM_STATE), F32), pltpu.VMEM((1, SSM_STATE), F32),
                        pltpu.VMEM((tile, W_BRANCH), F32)],
        compiler_params=_cparams(("parallel", "arbitrary")),
        name="ssm_prompt")
    return call(p3, p3, wb, wc, pwr, pwi, d, glu_w, glu_b, *extra)


def _decay_kernel(alr_ref, wa2_ref, ba_ref, a_ref):
    a_ref[...] = jnp.exp(_gla_log_decay(alr_ref[...], wa2_ref[...], ba_ref[...]))


def _sample_decay(p, w_a2p, b_a, l):
    n = p.shape[0]
    return pl.pallas_call(
        _decay_kernel,
        grid=(1,),
        in_specs=[pl.BlockSpec((n, LANE), lambda i: (0, ALR_BLOCK)), _layer((LANE, GLA_HK), l), _layer((1, GLA_HK), l)],
        out_specs=pl.BlockSpec((n, GLA_HK), lambda i: (0, 0)),
        out_shape=jax.ShapeDtypeStruct((n, GLA_HK), F32),
        name="sample_decay",
    )(p, w_a2p, b_a)


def _sample_kernel(at_ref, qt_ref, kt_ref, v_ref, ga_ref, ub_ref, vb_ref, gb_ref, cb_ref, cc_ref, hc_ref, gc_ref,
                   ud_ref, gd_ref, sgla_ref, sconv_ref, sre_ref, sim_ref,
                   glag_ref, sgug_ref, sguw_ref, sgub_ref, convw_ref,
                   wb_ref, wc_ref, pwr_ref, pwi_ref, d_ref, gluw_ref, glub_ref,
                   oa_ref, ob_ref, oc_ref, od_ref, ngla_ref, nconv_ref, nre_ref, nim_ref, vn_ref, orow_ref):
    rows = v_ref.shape[0]

    a_t = at_ref[...]
    q_t = qt_ref[...] * (GLA_DK ** -0.5)
    k_t = kt_ref[...]
    v = v_ref[...]
    for n in range(rows):
        for h in range(GLA_HEADS):
            keys = slice(h * GLA_DK, (h + 1) * GLA_DK)
            vals = slice(h * GLA_DV, (h + 1) * GLA_DV)
            s_new = a_t[keys, n:n + 1] * sgla_ref[n, h] + k_t[keys, n:n + 1] * v[n:n + 1, vals]
            ngla_ref[n, h] = s_new
            orow_ref[n:n + 1, vals] = jnp.sum(q_t[keys, n:n + 1] * s_new, axis=0, keepdims=True)
    o_a = _head_rmsnorm(orow_ref[...], glag_ref[...], GLA_HEADS, GLA_DV)
    oa_ref[...] = (o_a * _silu(ga_ref[...])).astype(oa_ref.dtype)

    v_n = _sgu_layernorm(vb_ref[...], sgug_ref[...])
    vn_ref[...] = v_n
    ob_ref[...] = (ub_ref[...] * (sguw_ref[...] * v_n + sgub_ref[...]) * _silu(gb_ref[...])).astype(ob_ref.dtype)

    z = cc_ref[...] * hc_ref[...]
    w = convw_ref[...]
    buf0, buf1 = sconv_ref[:, 0:W_BRANCH], sconv_ref[:, W_BRANCH:2 * W_BRANCH]
    y_c = w[0:1, :] * buf0 + w[1:2, :] * buf1 + w[2:3, :] * z
    oc_ref[...] = (cb_ref[...] * y_c * _silu(gc_ref[...])).astype(oc_ref.dtype)
    nconv_ref[:, 0:W_BRANCH] = buf1
    nconv_ref[:, W_BRANCH:2 * W_BRANCH] = z

    u = ud_ref[...]
    re, im = _ssm_in(u, wb_ref, _dot_exact)
    ys = []
    for s in range(SSM_SLABS):
        lanes = slice(s * SLAB_STATE, (s + 1) * SLAB_STATE)
        ar, ai = pwr_ref[0:1, lanes], pwi_ref[0:1, lanes]
        x0r, x0i = sre_ref[:, lanes], sim_ref[:, lanes]
        xr = re[s] + (ar * x0r - ai * x0i)
        xi = im[s] + (ar * x0i + ai * x0r)
        nre_ref[:, lanes] = xr
        nim_ref[:, lanes] = xi
        ys.append(_ssm_out(xr, xi, wc_ref, s, _dot_exact))
    y = jnp.concatenate(ys, axis=-1)
    od_ref[...] = _ssm_gate(y, u, gd_ref[...], d_ref[...], gluw_ref[...], glub_ref[...]).astype(od_ref.dtype)


def _sample_mixers(p, a, states, weights, l, prev):
    n = p.shape[0]
    depth = states[0].shape[0]
    rows = SAMPLE_ROWS
    to_cols = lambda m: m.reshape(n // rows, rows, GLA_HK).transpose(0, 2, 1)
    cols = pl.BlockSpec((None, GLA_HK, rows), lambda i: (i, 0, 0))
    col = lambda idx: pl.BlockSpec((rows, W_BRANCH), lambda i: (i, idx))
    wide = lambda width: pl.BlockSpec((rows, width), lambda i: (i, 0))
    lwide = lambda width: pl.BlockSpec((None, rows, width), lambda i: (l, i, 0))
    gla_state = pl.BlockSpec((None, rows, GLA_HEADS, GLA_DK, GLA_DV), lambda i: (l, i, 0, 0, 0))
    state_specs = [gla_state, lwide(2 * W_BRANCH), lwide(SSM_STATE), lwide(SSM_STATE)]
    p_cols = (COL_V, COL_GA, COL_UB, COL_VB, COL_GB, COL_CB, COL_CC, COL_HC, COL_GC, COL_UD, COL_GD)
    out_bf = jax.ShapeDtypeStruct((n, W_BRANCH), BF16)
    n_in = 3 + len(p_cols) + len(states) + len(weights)
    call, extra = _stacked_call(
        _sample_kernel, n_in, prev, 4,
        grid=(n // rows,),
        in_specs=[cols] * 3 + [col(idx) for idx in p_cols] + state_specs
        + [_layer(w.shape[1:], l) for w in weights],
        out_specs=[wide(W_BRANCH)] * 4 + state_specs + [lwide(W_BRANCH)],
        out_shape=(out_bf, out_bf, out_bf, out_bf)
        + tuple(jax.ShapeDtypeStruct(s.shape, F32) for s in states)
        + (jax.ShapeDtypeStruct((depth, n, W_BRANCH), F32),),
        scratch_shapes=[pltpu.VMEM((rows, W_BRANCH), F32)],
        compiler_params=_cparams(("parallel",)),
        name="sample_mixers")
    return call(to_cols(a), to_cols(p[:, :GLA_HK]), to_cols(p[:, GLA_HK:2 * GLA_HK]), *([p] * len(p_cols)),
                *states, *weights, *extra)


def kernel(x_prompt, x_sample, state_gla, state_conv, state_ssm_re, state_ssm_im, norm_g, w_in, w_a2, b_a, gla_g,
           sgu_g, sgu_w, sgu_b, conv_w, ssm_lambda_re, ssm_lambda_im, ssm_log_dt, ssm_b_re, ssm_b_im, ssm_c_re,
           ssm_c_im, ssm_d, glu_w, glu_b, w_out, final_norm_g):
    bsz, seq, _ = x_prompt.shape
    nsamp, dec_seq, _ = x_sample.shape
    depth = w_in.shape[0]
    assert dec_seq == 1 and nsamp % SAMPLE_ROWS == 0 and (bsz * seq) % INPROJ_ROWS == 0
    assert all(seq % tile == 0 for tile in (GLA_TILE, SGU_TILE, CONV_TILE, SSM_TILE))
    hp = x_prompt.reshape(bsz * seq, D_MODEL)
    hs = x_sample.reshape(nsamp, D_MODEL)

    vec = lambda a: a.reshape(depth, 1, a.shape[-1])
    w_in_p = _prep_w_in(w_in)
    w_out_b = w_out.astype(BF16)
    glu_w_b = glu_w.astype(BF16)
    norm_g3, b_a3, gla_g3, sgu_g3, d3, glu_b3 = map(vec, (norm_g, b_a, gla_g, sgu_g, ssm_d, glu_b))
    final_g = final_norm_g.reshape(1, D_MODEL)
    w_a2p = jnp.pad(w_a2, ((0, 0), (0, LANE - GLA_RANK), (0, 0)))
    sgu_b_t = sgu_b.transpose(0, 2, 1)
    sgu_w0 = vec(jnp.repeat(sgu_w[:, :, 0, 0], SGU_HD, axis=-1))
    sgu_b0 = vec(jnp.repeat(sgu_b[:, :, 0], SGU_HD, axis=-1))
    pwr, pwi, wb, wc = _ssm_matrices(ssm_lambda_re, ssm_lambda_im, ssm_log_dt, ssm_b_re, ssm_b_im,
                                     ssm_c_re, ssm_c_im)
    wb_b, wc_b = wb.astype(BF16), wc.astype(BF16)
    sample_states = (state_gla, state_conv.reshape(depth, nsamp, 2 * W_BRANCH),
                     state_ssm_re.reshape(depth, nsamp, SSM_STATE), state_ssm_im.reshape(depth, nsamp, SSM_STATE))
    sample_weights = (gla_g3, sgu_g3, sgu_w0, sgu_b0, conv_w, wb, wc, pwr, pwi, d3, glu_w_b, glu_b3)

    gla_p = conv_p = ssm_p = samp = None
    for l in range(depth):
        final = l == depth - 1

        p3 = _inproj(hp, norm_g3, w_in_p, l, INPROJ_ROWS).reshape(bsz, seq, PROJ_PAD)
        o_a, *gla_p = _gla_prompt(p3, w_a2p, b_a3, gla_g3, l, gla_p)
        o_b = _sgu_prompt(p3, sgu_g3, sgu_w, sgu_b_t, l)
        o_c, *conv_p = _conv_prompt(p3, conv_w, l, conv_p)
        o_d, *ssm_p = _ssm_prompt(p3, wb_b, wc_b, pwr, pwi, d3, glu_w_b, glu_b3, l, ssm_p)
        mixed = [o.reshape(bsz * seq, W_BRANCH) for o in (o_a, o_b, o_c, o_d)]
        hp = _outproj(hp, mixed, w_out_b, final_g, l, OUTPROJ_ROWS, final)

        ps = _inproj(hs, norm_g3, w_in_p, l, nsamp)
        s_a, s_b, s_c, s_d, *samp = _sample_mixers(ps, _sample_decay(ps, w_a2p, b_a3, l), sample_states,
                                                   sample_weights, l, samp)
        hs = _outproj(hs, [s_a, s_b, s_c, s_d], w_out_b, final_g, l, nsamp, final)

    gla_s, conv_s, re_s, im_s, vn_s = samp
    groups = lambda s: s.reshape(depth, -1, SSM_GROUPS, SSM_N)
    return (hp.reshape(bsz, seq, D_MODEL), hs.reshape(nsamp, 1, D_MODEL),
            gla_p[0], gla_s,
            conv_p[0].reshape(depth, bsz, 2, W_BRANCH), conv_s.reshape(depth, nsamp, 2, W_BRANCH),
            groups(ssm_p[0]), groups(ssm_p[1]), groups(re_s), groups(im_s),
            vn_s.reshape(depth, nsamp, 1, W_BRANCH))
```

```python
import functools

import jax
import jax.numpy as jnp
from jax import lax
from jax.experimental import pallas as pl
from jax.experimental.pallas import tpu as pltpu

F32 = jnp.float32
BF16 = jnp.bfloat16
HIGHEST = lax.Precision.HIGHEST

D_MODEL = 2048
W_BRANCH = 512
EPS = 1e-6
GLA_HEADS = 4
GLA_DK = 64
GLA_DV = 128
GLA_HK = GLA_HEADS * GLA_DK
GLA_RANK = 16
GLA_TAU = 16.0
SGU_HEADS = 4
SGU_HD = 128
SGU_CHUNK = 128
SSM_GROUPS = 32
SSM_GROUP_SIZE = 16
SSM_N = 64
SSM_STATE = SSM_GROUPS * SSM_N
SSM_TAPS = 4
PROJ_TOTAL = 6160

LANE = 128
SUBLANE = 8
MXU_TILE = 256
PROJ_PAD = 6400
ALR_SRC = 2 * GLA_HK + W_BRANCH
ALR_DST = PROJ_TOTAL - GLA_RANK
ALR_BLOCK = ALR_DST // LANE
COL_V, COL_GA, COL_UB, COL_VB, COL_GB, COL_CB, COL_CC, COL_HC, COL_GC, COL_UD, COL_GD = range(1, 12)
SSM_SLABS = W_BRANCH // LANE
SLAB_STATE = SSM_STATE // SSM_SLABS
GLA_CHUNK = 128
EXP_CLAMP = 80.0
VMEM_LIMIT = 56 * 1024 * 1024

INPROJ_ROWS, INPROJ_COLS = 1024, 5 * MXU_TILE
OUTPROJ_ROWS = 512
WPREP_ROWS = MXU_TILE
GLA_TILE = 512
SGU_TILE = 512
CONV_TILE = 512
SSM_TILE = 512
SAMPLE_ROWS = 16


def _cparams(sem):
    return pltpu.CompilerParams(dimension_semantics=sem, vmem_limit_bytes=VMEM_LIMIT)


def _silu(x):
    return x * jax.nn.sigmoid(x)


def _log_sigmoid(x):
    return jnp.minimum(x, 0.0) - jnp.log(1.0 + jnp.exp(-jnp.abs(x)))


def _dot(a, b):
    return jnp.dot(a, b, preferred_element_type=F32)


def _dot_bf16(a, b):
    return jnp.dot(a.astype(BF16), b.astype(BF16), preferred_element_type=F32)


def _dot_exact(a, b):
    return jnp.dot(a, b, preferred_element_type=F32, precision=HIGHEST)


def _split(x):
    hi = x.astype(BF16)
    return hi, (x - hi.astype(F32)).astype(BF16)


def _dot_split(a, b):
    a_hi, a_lo = _split(a)
    b_hi, b_lo = _split(b)
    return _dot(a_hi, b_hi) + (_dot(a_hi, b_lo) + _dot(a_lo, b_hi))


def _dot_nt(a, b):
    return lax.dot_general(a, b, (((1,), (1,)), ((), ())), preferred_element_type=F32)


def _layer(shape, l):
    return pl.BlockSpec((None,) + tuple(shape), lambda *_: (l,) + (0,) * len(shape))


def _drop_refs(fn, start, count):
    def wrapped(*refs):
        return fn(*refs[:start], *refs[start + count:])
    return wrapped


def _stacked_call(kernel_fn, n_in, prev, out_first, **kwargs):
    in_specs = list(kwargs.pop("in_specs"))
    aliases = {}
    if prev is not None:
        in_specs += [pl.BlockSpec(memory_space=pl.ANY)] * len(prev)
        aliases = {n_in + k: out_first + k for k in range(len(prev))}
        kernel_fn = _drop_refs(kernel_fn, n_in, len(prev))
    return pl.pallas_call(kernel_fn, in_specs=in_specs, input_output_aliases=aliases, **kwargs), \
        (() if prev is None else tuple(prev))


def _wprep_kernel(w_ref, o_ref):
    last = pl.program_id(1) == pl.num_programs(1) - 1

    @pl.when(jnp.logical_not(last))
    def _():
        o_ref[...] = w_ref[0].astype(BF16)

    @pl.when(last)
    def _():
        o_ref[...] = jnp.zeros_like(o_ref)
        o_ref[0:GLA_RANK, :] = w_ref[0, 0:GLA_RANK, :].astype(BF16)


def _prep_w_in(w_in):
    depth = w_in.shape[0]
    w_t = jnp.swapaxes(w_in, 1, 2)
    tr = WPREP_ROWS
    n_head = ALR_SRC // tr
    n_body = ALR_DST // tr

    def src_row(j):
        u = GLA_RANK
        return jnp.where(j < n_head, j * (tr // u), jnp.where(j < n_body, j * (tr // u) + 1, ALR_SRC // u)) * u

    return pl.pallas_call(
        _wprep_kernel,
        grid=(depth, PROJ_PAD // tr),
        in_specs=[pl.BlockSpec((pl.Element(1), pl.Element(tr), pl.Element(D_MODEL)),
                               lambda l, j: (l, src_row(j), 0))],
        out_specs=pl.BlockSpec((None, tr, D_MODEL), lambda l, j: (l, j, 0)),
        out_shape=jax.ShapeDtypeStruct((depth, PROJ_PAD, D_MODEL), BF16),
        compiler_params=_cparams(("parallel", "parallel")),
        name="prep_w_in",
    )(w_t)


def _inproj_kernel(x_ref, g_ref, w_ref, o_ref, h_ref):
    @pl.when(pl.program_id(1) == 0)
    def _():
        x = x_ref[...]
        ms = jnp.mean(x * x, axis=-1, keepdims=True)
        h_ref[...] = (x * lax.rsqrt(ms + EPS) * g_ref[...]).astype(BF16)

    o_ref[...] = _dot_nt(h_ref[...], w_ref[...])


def _inproj(x, g, w, l, tm):
    m = x.shape[0]
    tn = INPROJ_COLS
    return pl.pallas_call(
        _inproj_kernel,
        grid=(m // tm, PROJ_PAD // tn),
        in_specs=[pl.BlockSpec((tm, D_MODEL), lambda i, j: (i, 0)),
                  _layer((1, D_MODEL), l),
                  pl.BlockSpec((None, tn, D_MODEL), lambda i, j: (l, j, 0))],
        out_specs=pl.BlockSpec((tm, tn), lambda i, j: (i, j)),
        out_shape=jax.ShapeDtypeStruct((m, PROJ_PAD), F32),
        scratch_shapes=[pltpu.VMEM((tm, D_MODEL), BF16)],
        compiler_params=_cparams(("parallel", "arbitrary")),
        name="inproj",
    )(x, g, w)


def _outproj_kernel(x_ref, oa_ref, ob_ref, oc_ref, od_ref, w_ref, g_ref, y_ref, *, final):
    acc = x_ref[...]
    for i, o_ref in enumerate((oa_ref, ob_ref, oc_ref, od_ref)):
        acc = acc + _dot(o_ref[...], w_ref[i * W_BRANCH:(i + 1) * W_BRANCH, :])
    if final:
        ms = jnp.mean(acc * acc, axis=-1, keepdims=True)
        acc = acc * lax.rsqrt(ms + EPS) * g_ref[...]
    y_ref[...] = acc


def _outproj(x, outs, w, g, l, tm, final):
    m = x.shape[0]
    row = lambda i: (i, 0)
    return pl.pallas_call(
        functools.partial(_outproj_kernel, final=final),
        grid=(m // tm,),
        in_specs=[pl.BlockSpec((tm, D_MODEL), row)]
        + [pl.BlockSpec((tm, W_BRANCH), row)] * 4
        + [_layer((D_MODEL, D_MODEL), l), pl.BlockSpec((1, D_MODEL), lambda i: (0, 0))],
        out_specs=pl.BlockSpec((tm, D_MODEL), row),
        out_shape=jax.ShapeDtypeStruct((m, D_MODEL), F32),
        compiler_params=_cparams(("parallel",)),
        name="outproj",
    )(x, *outs, w, g)


def _head_rmsnorm(o, gain, heads, width):
    parts = []
    for h in range(heads):
        oh = o[:, h * width:(h + 1) * width]
        ms = jnp.mean(oh * oh, axis=-1, keepdims=True)
        parts.append(oh * lax.rsqrt(ms + EPS))
    return jnp.concatenate(parts, axis=-1) * gain


def _sgu_layernorm(v, gain):
    parts = []
    for h in range(SGU_HEADS):
        vh = v[:, h * SGU_HD:(h + 1) * SGU_HD]
        vh = vh - jnp.mean(vh, axis=-1, keepdims=True)
        parts.append(vh * lax.rsqrt(jnp.mean(vh * vh, axis=-1, keepdims=True) + EPS))
    return jnp.concatenate(parts, axis=-1) * gain


def _gla_log_decay(alr, w_a2, b_a):
    z = _dot_split(alr, w_a2) + b_a
    return _log_sigmoid(z) * (1.0 / GLA_TAU)


def _ssm_in(taps, wb_ref, dot):
    re, im = [], []
    for s in range(SSM_SLABS):
        lhs = jnp.concatenate([tap[:, s * LANE:(s + 1) * LANE] for tap in taps], axis=1)
        r = dot(lhs, wb_ref[s, 0:len(taps) * LANE, :])
        re.append(r[:, :SLAB_STATE])
        im.append(r[:, SLAB_STATE:])
    return re, im


def _ssm_out(xr, xi, wc_ref, s, dot):
    return dot(xr, wc_ref[s, :SLAB_STATE, :]) + dot(xi, wc_ref[s, SLAB_STATE:, :])


def _ssm_gate(y, u, gate, d, glu_w, glu_b):
    gd = jax.nn.gelu(y + d * u)
    return gd * jax.nn.sigmoid(_dot(gd.astype(BF16), glu_w) + glu_b) * _silu(gate)


def _ssm_prep_kernel(lr_ref, li_ref, ldt_ref, brt_ref, bit_ref, pwr_ref, pwi_ref, bbr_ref, bbi_ref):
    lr, li = lr_ref[...], li_ref[...]
    dt = jnp.exp(ldt_ref[...])
    for m in range(SUBLANE):
        mag = jnp.exp(lr * dt * (m + 1.0))
        pwr_ref[m] = mag * jnp.cos(li * dt * (m + 1.0))
        pwi_ref[m] = mag * jnp.sin(li * dt * (m + 1.0))
    ar, ai = pwr_ref[0], pwi_ref[0]
    den = lr * lr + li * li
    cr = ((ar - 1.0) * lr + ai * li) / den
    ci = (ai * lr - (ar - 1.0) * li) / den
    br, bi = brt_ref[...], bit_ref[...]
    b0r = cr * br - ci * bi
    b0i = cr * bi + ci * br
    bbr_ref[0] = b0r
    bbi_ref[0] = b0i
    for s in range(1, SSM_TAPS):
        pr, pi = pwr_ref[s - 1], pwi_ref[s - 1]
        bbr_ref[s] = pr * b0r - pi * b0i
        bbi_ref[s] = pr * b0i + pi * b0r


def _ssm_matrices(lam_re, lam_im, log_dt, b_re, b_im, c_re, c_im):
    depth = lam_re.shape[0]
    dg, n, p = depth * SSM_GROUPS, SSM_N, SSM_GROUP_SIZE
    pw_shape = jax.ShapeDtypeStruct((SUBLANE, dg, 1, n), F32)
    bb_shape = jax.ShapeDtypeStruct((SSM_TAPS, dg, p, n), F32)
    pwr, pwi, bbr_t, bbi_t = pl.pallas_call(
        _ssm_prep_kernel, out_shape=(pw_shape, pw_shape, bb_shape, bb_shape), name="ssm_prep",
    )(lam_re.reshape(dg, 1, n), lam_im.reshape(dg, 1, n), log_dt.reshape(dg, 1, 1),
      b_re.transpose(0, 1, 3, 2).reshape(dg, p, n), b_im.transpose(0, 1, 3, 2).reshape(dg, p, n))
    gs = SSM_GROUPS // SSM_SLABS
    eye = jnp.eye(gs, dtype=F32)

    def block_diag(blocks):
        *lead, _, r, c = blocks.shape
        return (blocks[..., :, :, None, :] * eye[:, None, :, None]).reshape(*lead, gs * r, gs * c)

    def tap_rows(bb):
        bd = block_diag(bb.reshape(SSM_TAPS, depth, SSM_SLABS, gs, p, n))
        return bd.transpose(1, 2, 0, 3, 4).reshape(depth, SSM_SLABS, SSM_TAPS * LANE, SLAB_STATE)

    wb = jnp.concatenate([tap_rows(bbr_t), tap_rows(bbi_t)], axis=-1)
    shape_c = (depth, SSM_SLABS, gs, n, p)
    wc = jnp.concatenate([block_diag(c_re.transpose(0, 1, 3, 2).reshape(shape_c)),
                          block_diag(-c_im.transpose(0, 1, 3, 2).reshape(shape_c))], axis=2)
    powers = lambda pw: pw.reshape(SUBLANE, depth, SSM_STATE).transpose(1, 0, 2)
    return powers(pwr), powers(pwi), wb, wc


def _gla_kernel(q_ref, k_ref, v_ref, ga_ref, alr_ref, wa2_ref, ba_ref, gain_ref, o_ref, s_ref, st_ref):
    c = GLA_CHUNK
    i = pl.program_id(1)

    @pl.when(i == 0)
    def _():
        st_ref[...] = jnp.zeros_like(st_ref)

    row = lax.broadcasted_iota(jnp.int32, (c, c), 0)
    col = lax.broadcasted_iota(jnp.int32, (c, c), 1)
    tril = jnp.where(col <= row, 1.0, 0.0).astype(BF16)
    key_head = lax.broadcasted_iota(jnp.int32, (c, GLA_HK), 1) // GLA_DK
    val_head = lax.broadcasted_iota(jnp.int32, (c, W_BRANCH), 1) // GLA_DV
    t_idx = lax.broadcasted_iota(jnp.int32, (c, GLA_HEADS * c), 0)
    s_idx = lax.broadcasted_iota(jnp.int32, (c, GLA_HEADS * c), 1) % c
    st_row_head = lax.broadcasted_iota(jnp.int32, st_ref.shape, 0) // GLA_DV
    st_col_head = lax.broadcasted_iota(jnp.int32, st_ref.shape, 1) // GLA_DK
    same_head = st_row_head == st_col_head

    key_masks = [jnp.where(key_head == h, 1.0, 0.0).astype(BF16) for h in range(GLA_HEADS)]
    val_masks = [jnp.where(val_head == h, 1.0, 0.0).astype(BF16) for h in range(GLA_HEADS)]
    chunks = [slice(r0, r0 + c) for r0 in range(0, q_ref.shape[0], c)]

    la = _gla_log_decay(alr_ref[...], wa2_ref[...], ba_ref[...])
    qes, kds, decays, scores = [], [], [], []
    for rows in chunks:
        la_hi, la_lo = _split(la[rows, :])
        b = _dot(tril, la_hi) + _dot(tril, la_lo)
        b_last = b[c - 1:c, :]
        k = k_ref[rows, :]
        qe = (q_ref[rows, :] * (GLA_DK ** -0.5) * jnp.exp(b)).astype(BF16)
        ke = (k * jnp.exp(jnp.minimum(-b, EXP_CLAMP))).astype(BF16)
        k_bd = jnp.concatenate([ke * m for m in key_masks], axis=0)
        sc = _dot_nt(qe, k_bd)
        scores.append(jnp.where(s_idx <= t_idx, sc, 0.0).astype(BF16))
        qes.append(qe)
        kds.append((k * jnp.exp(b_last - b)).astype(BF16))
        decays.append(jnp.exp(b_last))
    intra, upds = [], []
    for n, rows in enumerate(chunks):
        v = v_ref[rows, :]
        vb = v.astype(BF16)
        v_bd = jnp.concatenate([vb * m for m in val_masks], axis=0)
        intra.append(_dot(scores[n], v_bd))
        upds.append(jnp.where(same_head, _dot(v.T.astype(BF16), kds[n]), 0.0))
    st = st_ref[...]
    outs = []
    for n in range(len(chunks)):
        outs.append(intra[n] + _dot_nt(qes[n], st.astype(BF16)))
        st = st * decays[n] + upds[n]
    st_ref[...] = st
    for n, rows in enumerate(chunks):
        o = _head_rmsnorm(outs[n], gain_ref[...], GLA_HEADS, GLA_DV)
        o_ref[rows, :] = (o * _silu(ga_ref[rows, :])).astype(o_ref.dtype)

    @pl.when(i == pl.num_programs(1) - 1)
    def _():
        for h in range(GLA_HEADS):
            slab_t = st[h * GLA_DV:(h + 1) * GLA_DV, :].T
            s_ref[h] = slab_t[h * GLA_DK:(h + 1) * GLA_DK, :]


def _gla_prompt(p3, w_a2p, b_a, gain, l, prev):
    bsz, seq, _ = p3.shape
    depth = w_a2p.shape[0]
    c = GLA_TILE
    blk = lambda width, idx: pl.BlockSpec((None, c, width), lambda b, i: (b, i, idx))
    call, extra = _stacked_call(
        _gla_kernel, 8, prev, 1,
        grid=(bsz, seq // c),
        in_specs=[blk(GLA_HK, 0), blk(GLA_HK, 1), blk(W_BRANCH, COL_V), blk(W_BRANCH, COL_GA), blk(LANE, ALR_BLOCK),
                  _layer((LANE, GLA_HK), l), _layer((1, GLA_HK), l), _layer((1, W_BRANCH), l)],
        out_specs=[pl.BlockSpec((None, c, W_BRANCH), lambda b, i: (b, i, 0)),
                   pl.BlockSpec((None, None, GLA_HEADS, GLA_DK, GLA_DV), lambda b, i: (l, b, 0, 0, 0))],
        out_shape=(jax.ShapeDtypeStruct((bsz, seq, W_BRANCH), BF16),
                   jax.ShapeDtypeStruct((depth, bsz, GLA_HEADS, GLA_DK, GLA_DV), F32)),
        scratch_shapes=[pltpu.VMEM((GLA_HEADS * GLA_DV, GLA_HK), F32)],
        compiler_params=_cparams(("parallel", "arbitrary")),
        name="gla_prompt")
    return call(p3, p3, p3, p3, p3, w_a2p, b_a, gain, *extra)


def _sgu_kernel(u_ref, v_ref, g_ref, gain_ref, w_ref, bt_ref, o_ref):
    c = SGU_CHUNK
    row = lax.broadcasted_iota(jnp.int32, (c, c), 0)
    col = lax.broadcasted_iota(jnp.int32, (c, c), 1)
    w = [jnp.where(col <= row, w_ref[h], 0.0).astype(BF16) for h in range(SGU_HEADS)]
    for r0 in range(0, u_ref.shape[0], c):
        rows = slice(r0, r0 + c)
        v_n = _sgu_layernorm(v_ref[rows, :], gain_ref[...])
        parts = []
        for h in range(SGU_HEADS):
            mixed = _dot(w[h], v_n[:, h * SGU_HD:(h + 1) * SGU_HD].astype(BF16))
            parts.append(mixed + bt_ref[:, h:h + 1])
        mixed = jnp.concatenate(parts, axis=-1)
        o_ref[rows, :] = (u_ref[rows, :] * mixed * _silu(g_ref[rows, :])).astype(o_ref.dtype)


def _sgu_prompt(p3, gain, sgu_w, sgu_b_t, l):
    bsz, seq, _ = p3.shape
    c = SGU_CHUNK
    tile = SGU_TILE
    blk = lambda idx: pl.BlockSpec((None, tile, W_BRANCH), lambda b, i: (b, i, idx))
    return pl.pallas_call(
        _sgu_kernel,
        grid=(bsz, seq // tile),
        in_specs=[blk(COL_UB), blk(COL_VB), blk(COL_GB), _layer((1, W_BRANCH), l),
                  _layer((SGU_HEADS, c, c), l), _layer((c, SGU_HEADS), l)],
        out_specs=pl.BlockSpec((None, tile, W_BRANCH), lambda b, i: (b, i, 0)),
        out_shape=jax.ShapeDtypeStruct((bsz, seq, W_BRANCH), BF16),
        compiler_params=_cparams(("parallel", "parallel")),
        name="sgu_prompt",
    )(p3, p3, p3, gain, sgu_w, sgu_b_t)


def _conv_kernel(cb_ref, cc_ref, hc_ref, g_ref, w_ref, o_ref, new_ref, zbuf_ref):
    t = cb_ref.shape[0]
    i = pl.program_id(1)

    @pl.when(i == 0)
    def _():
        zbuf_ref[0:SUBLANE, :] = jnp.zeros((SUBLANE, W_BRANCH), F32)

    z = cc_ref[...] * hc_ref[...]
    zbuf_ref[SUBLANE:SUBLANE + t, :] = z
    w = w_ref[...]
    y = w[0:1, :] * zbuf_ref[SUBLANE - 2:SUBLANE - 2 + t, :]
    y = y + w[1:2, :] * zbuf_ref[SUBLANE - 1:SUBLANE - 1 + t, :]
    y = y + w[2:3, :] * z
    o_ref[...] = (cb_ref[...] * y * _silu(g_ref[...])).astype(o_ref.dtype)
    zbuf_ref[0:SUBLANE, :] = z[t - SUBLANE:t, :]

    @pl.when(i == pl.num_programs(1) - 1)
    def _():
        new_ref[:, 0:W_BRANCH] = z[t - 2:t - 1, :]
        new_ref[:, W_BRANCH:2 * W_BRANCH] = z[t - 1:t, :]


def _conv_prompt(p3, conv_w, l, prev):
    bsz, seq, _ = p3.shape
    depth = conv_w.shape[0]
    tile = CONV_TILE
    blk = lambda idx: pl.BlockSpec((None, tile, W_BRANCH), lambda b, i: (b, i, idx))
    call, extra = _stacked_call(
        _conv_kernel, 5, prev, 1,
        grid=(bsz, seq // tile),
        in_specs=[blk(COL_CB), blk(COL_CC), blk(COL_HC), blk(COL_GC), _layer((3, W_BRANCH), l)],
        out_specs=[pl.BlockSpec((None, tile, W_BRANCH), lambda b, i: (b, i, 0)),
                   pl.BlockSpec((None, None, 1, 2 * W_BRANCH), lambda b, i: (l, b, 0, 0))],
        out_shape=(jax.ShapeDtypeStruct((bsz, seq, W_BRANCH), BF16),
                   jax.ShapeDtypeStruct((depth, bsz, 1, 2 * W_BRANCH), F32)),
        scratch_shapes=[pltpu.VMEM((tile + SUBLANE, W_BRANCH), F32)],
        compiler_params=_cparams(("parallel", "arbitrary")),
        name="conv_prompt")
    return call(p3, p3, p3, p3, conv_w, *extra)


def _ssm_kernel(u_ref, g_ref, wb_ref, wc_ref, pwr_ref, pwi_ref, d_ref, gluw_ref, glub_ref,
                o_ref, sr_ref, si_ref, ubuf_ref, xr_ref, xi_ref, cr_ref, ci_ref, y_ref):
    t = u_ref.shape[0]
    i = pl.program_id(1)
    half = SUBLANE // 2

    @pl.when(i == 0)
    def _():
        ubuf_ref[0:SUBLANE, :] = jnp.zeros((SUBLANE, W_BRANCH), F32)
        cr_ref[...] = jnp.zeros_like(cr_ref)
        ci_ref[...] = jnp.zeros_like(ci_ref)

    u = u_ref[...]
    ubuf_ref[SUBLANE:SUBLANE + t, :] = u
    taps = [u.astype(BF16)]
    taps += [ubuf_ref[SUBLANE - s:SUBLANE - s + t, :].astype(BF16) for s in range(1, SSM_TAPS)]
    ubuf_ref[0:SUBLANE, :] = u[t - SUBLANE:t, :]
    re, im = _ssm_in(taps, wb_ref, _dot)
    for s in range(SSM_SLABS):
        xr_ref[:, s * SLAB_STATE:(s + 1) * SLAB_STATE] = re[s]
        xi_ref[:, s * SLAB_STATE:(s + 1) * SLAB_STATE] = im[s]

    row = lax.broadcasted_iota(jnp.int32, (SUBLANE, SLAB_STATE), 0)
    for s in range(SSM_SLABS):
        lanes = slice(s * SLAB_STATE, (s + 1) * SLAB_STATE)
        p4r, p4i = pwr_ref[half - 1:half, lanes], pwi_ref[half - 1:half, lanes]
        lo_r, lo_i = jnp.where(row < half, p4r, 0.0), jnp.where(row < half, p4i, 0.0)
        hi_r, hi_i = jnp.where(row >= half, p4r, 0.0), jnp.where(row >= half, p4i, 0.0)
        xr, xi = cr_ref[:, lanes], ci_ref[:, lanes]
        for j in range(t // SUBLANE):
            rows = slice(j * SUBLANE, (j + 1) * SUBLANE)
            qr, qi = pltpu.roll(xr, half, axis=0), pltpu.roll(xi, half, axis=0)
            yr = xr_ref[rows, lanes] + (lo_r * qr - lo_i * qi)
            yi = xi_ref[rows, lanes] + (lo_r * qi + lo_i * qr)
            qr, qi = pltpu.roll(yr, half, axis=0), pltpu.roll(yi, half, axis=0)
            xr = yr + (hi_r * qr - hi_i * qi)
            xi = yi + (hi_r * qi + hi_i * qr)
            xr_ref[rows, lanes] = xr
            xi_ref[rows, lanes] = xi
        cr_ref[:, lanes] = xr
        ci_ref[:, lanes] = xi
        y_ref[:, s * LANE:(s + 1) * LANE] = _ssm_out(xr_ref[:, lanes].astype(BF16), xi_ref[:, lanes].astype(BF16),
                                                     wc_ref, s, _dot)

    o = _ssm_gate(y_ref[...], u, g_ref[...], d_ref[...], gluw_ref[...], glub_ref[...])
    o_ref[...] = o.astype(o_ref.dtype)

    @pl.when(i == pl.num_programs(1) - 1)
    def _():
        sr_ref[...] = cr_ref[SUBLANE - 1:SUBLANE, :]
        si_ref[...] = ci_ref[SUBLANE - 1:SUBLANE, :]


def _ssm_prompt(p3, wb, wc, pwr, pwi, d, glu_w, glu_b, l, prev):
    bsz, seq, _ = p3.shape
    depth = wb.shape[0]
    tile = SSM_TILE
    blk = lambda idx: pl.BlockSpec((None, tile, W_BRANCH), lambda b, i: (b, i, idx))
    state = pl.BlockSpec((None, None, 1, SSM_STATE), lambda b, i: (l, b, 0, 0))
    state_shape = jax.ShapeDtypeStruct((depth, bsz, 1, SSM_STATE), F32)
    call, extra = _stacked_call(
        _ssm_kernel, 9, prev, 1,
        grid=(bsz, seq // tile),
        in_specs=[blk(COL_UD), blk(COL_GD), _layer(wb.shape[1:], l), _layer(wc.shape[1:], l),
                  _layer(pwr.shape[1:], l), _layer(pwi.shape[1:], l),
                  _layer((1, W_BRANCH), l), _layer((W_BRANCH, W_BRANCH), l), _layer((1, W_BRANCH), l)],
        out_specs=[pl.BlockSpec((None, tile, W_BRANCH), lambda b, i: (b, i, 0)), state, state],
        out_shape=(jax.ShapeDtypeStruct((bsz, seq, W_BRANCH), BF16), state_shape, state_shape),
        scratch_shapes=[pltpu.VMEM((tile + SUBLANE, W_BRANCH), F32),
                        pltpu.VMEM((tile, SSM_STATE), F32), pltpu.VMEM((tile, SSM_STATE), F32),
                        pltpu.VMEM((SUBLANE, SSM_STATE), F32), pltpu.VMEM((SUBLANE, SSM_STATE), F32),
                        pltpu.VMEM((tile, W_BRANCH), F32)],
        compiler_params=_cparams(("parallel", "arbitrary")),
        name="ssm_prompt")
    return call(p3, p3, wb, wc, pwr, pwi, d, glu_w, glu_b, *extra)


def _decay_kernel(alr_ref, wa2_ref, ba_ref, a_ref):
    a_ref[...] = jnp.exp(_gla_log_decay(alr_ref[...], wa2_ref[...], ba_ref[...]))


def _sample_decay(p, w_a2p, b_a, l):
    n = p.shape[0]
    return pl.pallas_call(
        _decay_kernel,
        grid=(1,),
        in_specs=[pl.BlockSpec((n, LANE), lambda i: (0, ALR_BLOCK)), _layer((LANE, GLA_HK), l), _layer((1, GLA_HK), l)],
        out_specs=pl.BlockSpec((n, GLA_HK), lambda i: (0, 0)),
        out_shape=jax.ShapeDtypeStruct((n, GLA_HK), F32),
        name="sample_decay",
    )(p, w_a2p, b_a)


def _sample_kernel(at_ref, qt_ref, kt_ref, v_ref, ga_ref, ub_ref, vb_ref, gb_ref, cb_ref, cc_ref, hc_ref, gc_ref,
                   ud_ref, gd_ref, sgla_ref, sconv_ref, sre_ref, sim_ref,
                   glag_ref, sgug_ref, sguw_ref, sgub_ref, convw_ref,
                   wb_ref, wc_ref, pwr_ref, pwi_ref, d_ref, gluw_ref, glub_ref,
                   oa_ref, ob_ref, oc_ref, od_ref, ngla_ref, nconv_ref, nre_ref, nim_ref, vn_ref, orow_ref):
    rows = v_ref.shape[0]

    a_t = at_ref[...]
    q_t = qt_ref[...] * (GLA_DK ** -0.5)
    k_t = kt_ref[...]
    v = v_ref[...]
    for n in range(rows):
        for h in range(GLA_HEADS):
            keys = slice(h * GLA_DK, (h + 1) * GLA_DK)
            vals = slice(h * GLA_DV, (h + 1) * GLA_DV)
            s_new = a_t[keys, n:n + 1] * sgla_ref[n, h] + k_t[keys, n:n + 1] * v[n:n + 1, vals]
            ngla_ref[n, h] = s_new
            orow_ref[n:n + 1, vals] = jnp.sum(q_t[keys, n:n + 1] * s_new, axis=0, keepdims=True)
    o_a = _head_rmsnorm(orow_ref[...], glag_ref[...], GLA_HEADS, GLA_DV)
    oa_ref[...] = (o_a * _silu(ga_ref[...])).astype(oa_ref.dtype)

    v_n = _sgu_layernorm(vb_ref[...], sgug_ref[...])
    vn_ref[...] = v_n
    ob_ref[...] = (ub_ref[...] * (sguw_ref[...] * v_n + sgub_ref[...]) * _silu(gb_ref[...])).astype(ob_ref.dtype)

    z = cc_ref[...] * hc_ref[...]
    w = convw_ref[...]
    buf0, buf1 = sconv_ref[:, 0:W_BRANCH], sconv_ref[:, W_BRANCH:2 * W_BRANCH]
    y_c = w[0:1, :] * buf0 + w[1:2, :] * buf1 + w[2:3, :] * z
    oc_ref[...] = (cb_ref[...] * y_c * _silu(gc_ref[...])).astype(oc_ref.dtype)
    nconv_ref[:, 0:W_BRANCH] = buf1
    nconv_ref[:, W_BRANCH:2 * W_BRANCH] = z

    u = ud_ref[...]
    re, im = _ssm_in([u], wb_ref, _dot_exact)
    ys = []
    for s in range(SSM_SLABS):
        lanes = slice(s * SLAB_STATE, (s + 1) * SLAB_STATE)
        ar, ai = pwr_ref[0:1, lanes], pwi_ref[0:1, lanes]
        x0r, x0i = sre_ref[:, lanes], sim_ref[:, lanes]
        xr = re[s] + (ar * x0r - ai * x0i)
        xi = im[s] + (ar * x0i + ai * x0r)
        nre_ref[:, lanes] = xr
        nim_ref[:, lanes] = xi
        ys.append(_ssm_out(xr, xi, wc_ref, s, _dot_exact))
    y = jnp.concatenate(ys, axis=-1)
    od_ref[...] = _ssm_gate(y, u, gd_ref[...], d_ref[...], gluw_ref[...], glub_ref[...]).astype(od_ref.dtype)


def _sample_mixers(p, a, states, weights, l, prev):
    n = p.shape[0]
    depth = states[0].shape[0]
    rows = SAMPLE_ROWS
    to_cols = lambda m: m.reshape(n // rows, rows, GLA_HK).transpose(0, 2, 1)
    cols = pl.BlockSpec((None, GLA_HK, rows), lambda i: (i, 0, 0))
    col = lambda idx: pl.BlockSpec((rows, W_BRANCH), lambda i: (i, idx))
    wide = lambda width: pl.BlockSpec((rows, width), lambda i: (i, 0))
    lwide = lambda width: pl.BlockSpec((None, rows, width), lambda i: (l, i, 0))
    gla_state = pl.BlockSpec((None, rows, GLA_HEADS, GLA_DK, GLA_DV), lambda i: (l, i, 0, 0, 0))
    state_specs = [gla_state, lwide(2 * W_BRANCH), lwide(SSM_STATE), lwide(SSM_STATE)]
    p_cols = (COL_V, COL_GA, COL_UB, COL_VB, COL_GB, COL_CB, COL_CC, COL_HC, COL_GC, COL_UD, COL_GD)
    out_bf = jax.ShapeDtypeStruct((n, W_BRANCH), BF16)
    n_in = 3 + len(p_cols) + len(states) + len(weights)
    call, extra = _stacked_call(
        _sample_kernel, n_in, prev, 4,
        grid=(n // rows,),
        in_specs=[cols] * 3 + [col(idx) for idx in p_cols] + state_specs
        + [_layer(w.shape[1:], l) for w in weights],
        out_specs=[wide(W_BRANCH)] * 4 + state_specs + [lwide(W_BRANCH)],
        out_shape=(out_bf, out_bf, out_bf, out_bf)
        + tuple(jax.ShapeDtypeStruct(s.shape, F32) for s in states)
        + (jax.ShapeDtypeStruct((depth, n, W_BRANCH), F32),),
        scratch_shapes=[pltpu.VMEM((rows, W_BRANCH), F32)],
        compiler_params=_cparams(("parallel",)),
        name="sample_mixers")
    return call(to_cols(a), to_cols(p[:, :GLA_HK]), to_cols(p[:, GLA_HK:2 * GLA_HK]), *([p] * len(p_cols)),
                *states, *weights, *extra)


def kernel(x_prompt, x_sample, state_gla, state_conv, state_ssm_re, state_ssm_im, norm_g, w_in, w_a2, b_a, gla_g,
           sgu_g, sgu_w, sgu_b, conv_w, ssm_lambda_re, ssm_lambda_im, ssm_log_dt, ssm_b_re, ssm_b_im, ssm_c_re,
           ssm_c_im, ssm_d, glu_w, glu_b, w_out, final_norm_g):
    bsz, seq, _ = x_prompt.shape
    nsamp, dec_seq, _ = x_sample.shape
    depth = w_in.shape[0]
    assert dec_seq == 1 and nsamp % SAMPLE_ROWS == 0 and (bsz * seq) % INPROJ_ROWS == 0
    assert all(seq % tile == 0 for tile in (GLA_TILE, SGU_TILE, CONV_TILE, SSM_TILE))
    hp = x_prompt.reshape(bsz * seq, D_MODEL)
    hs = x_sample.reshape(nsamp, D_MODEL)

    vec = lambda a: a.reshape(depth, 1, a.shape[-1])
    w_in_p = _prep_w_in(w_in)
    w_out_b = w_out.astype(BF16)
    glu_w_b = glu_w.astype(BF16)
    norm_g3, b_a3, gla_g3, sgu_g3, d3, glu_b3 = map(vec, (norm_g, b_a, gla_g, sgu_g, ssm_d, glu_b))
    final_g = final_norm_g.reshape(1, D_MODEL)
    w_a2p = jnp.pad(w_a2, ((0, 0), (0, LANE - GLA_RANK), (0, 0)))
    sgu_b_t = sgu_b.transpose(0, 2, 1)
    sgu_w0 = vec(jnp.repeat(sgu_w[:, :, 0, 0], SGU_HD, axis=-1))
    sgu_b0 = vec(jnp.repeat(sgu_b[:, :, 0], SGU_HD, axis=-1))
    pwr, pwi, wb, wc = _ssm_matrices(ssm_lambda_re, ssm_lambda_im, ssm_log_dt, ssm_b_re, ssm_b_im,
                                     ssm_c_re, ssm_c_im)
    wb_b, wc_b = wb.astype(BF16), wc.astype(BF16)
    sample_states = (state_gla, state_conv.reshape(depth, nsamp, 2 * W_BRANCH),
                     state_ssm_re.reshape(depth, nsamp, SSM_STATE), state_ssm_im.reshape(depth, nsamp, SSM_STATE))
    sample_weights = (gla_g3, sgu_g3, sgu_w0, sgu_b0, conv_w, wb, wc, pwr, pwi, d3, glu_w_b, glu_b3)

    gla_p = conv_p = ssm_p = samp = None
    for l in range(depth):
        final = l == depth - 1

        p3 = _inproj(hp, norm_g3, w_in_p, l, INPROJ_ROWS).reshape(bsz, seq, PROJ_PAD)
        o_a, *gla_p = _gla_prompt(p3, w_a2p, b_a3, gla_g3, l, gla_p)
        o_b = _sgu_prompt(p3, sgu_g3, sgu_w, sgu_b_t, l)
        o_c, *conv_p = _conv_prompt(p3, conv_w, l, conv_p)
        o_d, *ssm_p = _ssm_prompt(p3, wb_b, wc_b, pwr, pwi, d3, glu_w_b, glu_b3, l, ssm_p)
        mixed = [o.reshape(bsz * seq, W_BRANCH) for o in (o_a, o_b, o_c, o_d)]
        hp = _outproj(hp, mixed, w_out_b, final_g, l, OUTPROJ_ROWS, final)

        ps = _inproj(hs, norm_g3, w_in_p, l, nsamp)
        s_a, s_b, s_c, s_d, *samp = _sample_mixers(ps, _sample_decay(ps, w_a2p, b_a3, l), sample_states,
                                                   sample_weights, l, samp)
        hs = _outproj(hs, [s_a, s_b, s_c, s_d], w_out_b, final_g, l, nsamp, final)

    gla_s, conv_s, re_s, im_s, vn_s = samp
    groups = lambda s: s.reshape(depth, -1, SSM_GROUPS, SSM_N)
    return (hp.reshape(bsz, seq, D_MODEL), hs.reshape(nsamp, 1, D_MODEL),
            gla_p[0], gla_s,
            conv_p[0].reshape(depth, bsz, 2, W_BRANCH), conv_s.reshape(depth, nsamp, 2, W_BRANCH),
            groups(ssm_p[0]), groups(ssm_p[1]), groups(re_s), groups(im_s),
            vn_s.reshape(depth, nsamp, 1, W_BRANCH))
```

```python
import functools

import jax
import jax.numpy as jnp
from jax import lax
from jax.experimental import pallas as pl
from jax.experimental.pallas import tpu as pltpu

F32 = jnp.float32
BF16 = jnp.bfloat16
HIGHEST = lax.Precision.HIGHEST

D_MODEL = 2048
W_BRANCH = 512
EPS = 1e-6
GLA_HEADS = 4
GLA_DK = 64
GLA_DV = 128
GLA_HK = GLA_HEADS * GLA_DK
GLA_RANK = 16
GLA_TAU = 16.0
SGU_HEADS = 4
SGU_HD = 128
SGU_CHUNK = 128
SSM_GROUPS = 32
SSM_GROUP_SIZE = 16
SSM_N = 64
SSM_STATE = SSM_GROUPS * SSM_N
SSM_TAPS = 4
SSM_POWERS = (1, SSM_TAPS)
PROJ_TOTAL = 6160

LANE = 128
SUBLANE = 8
MXU_TILE = 256
PROJ_PAD = 6400
ALR_SRC = 2 * GLA_HK + W_BRANCH
ALR_DST = PROJ_TOTAL - GLA_RANK
ALR_BLOCK = ALR_DST // LANE
COL_V, COL_GA, COL_UB, COL_VB, COL_GB, COL_CB, COL_CC, COL_HC, COL_GC, COL_UD, COL_GD = range(1, 12)
SSM_SLABS = W_BRANCH // LANE
SLAB_STATE = SSM_STATE // SSM_SLABS
GLA_CHUNK = 128
EXP_CLAMP = 80.0
VMEM_LIMIT = 56 * 1024 * 1024

INPROJ_ROWS, INPROJ_COLS = 1024, 5 * MXU_TILE
OUTPROJ_ROWS = 512
WPREP_ROWS = MXU_TILE
GLA_TILE = 512
SGU_TILE = 512
CONV_TILE = 512
SSM_TILE = 512
SAMPLE_ROWS = 16


def _cparams(sem):
    return pltpu.CompilerParams(dimension_semantics=sem, vmem_limit_bytes=VMEM_LIMIT)


def _silu(x):
    return x * jax.nn.sigmoid(x)


def _log_sigmoid(x):
    return jnp.minimum(x, 0.0) - jnp.log(1.0 + jnp.exp(-jnp.abs(x)))


def _dot(a, b):
    return jnp.dot(a, b, preferred_element_type=F32)


def _dot_bf16(a, b):
    return jnp.dot(a.astype(BF16), b.astype(BF16), preferred_element_type=F32)


def _dot_exact(a, b):
    return jnp.dot(a, b, preferred_element_type=F32, precision=HIGHEST)


def _split(x):
    hi = x.astype(BF16)
    return hi, (x - hi.astype(F32)).astype(BF16)


def _dot_split(a, b):
    a_hi, a_lo = _split(a)
    b_hi, b_lo = _split(b)
    return _dot(a_hi, b_hi) + (_dot(a_hi, b_lo) + _dot(a_lo, b_hi))


def _dot_nt(a, b):
    return lax.dot_general(a, b, (((1,), (1,)), ((), ())), preferred_element_type=F32)


def _layer(shape, l):
    return pl.BlockSpec((None,) + tuple(shape), lambda *_: (l,) + (0,) * len(shape))


def _drop_refs(fn, start, count):
    def wrapped(*refs):
        return fn(*refs[:start], *refs[start + count:])
    return wrapped


def _stacked_call(kernel_fn, n_in, prev, out_first, **kwargs):
    in_specs = list(kwargs.pop("in_specs"))
    aliases = {}
    if prev is not None:
        in_specs += [pl.BlockSpec(memory_space=pl.ANY)] * len(prev)
        aliases = {n_in + k: out_first + k for k in range(len(prev))}
        kernel_fn = _drop_refs(kernel_fn, n_in, len(prev))
    return pl.pallas_call(kernel_fn, in_specs=in_specs, input_output_aliases=aliases, **kwargs), \
        (() if prev is None else tuple(prev))


def _wprep_kernel(w_ref, o_ref):
    last = pl.program_id(1) == pl.num_programs(1) - 1

    @pl.when(jnp.logical_not(last))
    def _():
        o_ref[...] = w_ref[0].astype(BF16)

    @pl.when(last)
    def _():
        o_ref[...] = jnp.zeros_like(o_ref)
        o_ref[0:GLA_RANK, :] = w_ref[0, 0:GLA_RANK, :].astype(BF16)


def _prep_w_in(w_in):
    depth = w_in.shape[0]
    w_t = jnp.swapaxes(w_in, 1, 2)
    tr = WPREP_ROWS
    n_head = ALR_SRC // tr
    n_body = ALR_DST // tr

    def src_row(j):
        u = GLA_RANK
        return jnp.where(j < n_head, j * (tr // u), jnp.where(j < n_body, j * (tr // u) + 1, ALR_SRC // u)) * u

    return pl.pallas_call(
        _wprep_kernel,
        grid=(depth, PROJ_PAD // tr),
        in_specs=[pl.BlockSpec((pl.Element(1), pl.Element(tr), pl.Element(D_MODEL)),
                               lambda l, j: (l, src_row(j), 0))],
        out_specs=pl.BlockSpec((None, tr, D_MODEL), lambda l, j: (l, j, 0)),
        out_shape=jax.ShapeDtypeStruct((depth, PROJ_PAD, D_MODEL), BF16),
        compiler_params=_cparams(("parallel", "parallel")),
        name="prep_w_in",
    )(w_t)


def _inproj_kernel(x_ref, g_ref, w_ref, o_ref, h_ref):
    @pl.when(pl.program_id(1) == 0)
    def _():
        x = x_ref[...]
        ms = jnp.mean(x * x, axis=-1, keepdims=True)
        h_ref[...] = (x * lax.rsqrt(ms + EPS) * g_ref[...]).astype(BF16)

    o_ref[...] = _dot_nt(h_ref[...], w_ref[...])


def _inproj(x, g, w, l, tm):
    m = x.shape[0]
    tn = INPROJ_COLS
    return pl.pallas_call(
        _inproj_kernel,
        grid=(m // tm, PROJ_PAD // tn),
        in_specs=[pl.BlockSpec((tm, D_MODEL), lambda i, j: (i, 0)),
                  _layer((1, D_MODEL), l),
                  pl.BlockSpec((None, tn, D_MODEL), lambda i, j: (l, j, 0))],
        out_specs=pl.BlockSpec((tm, tn), lambda i, j: (i, j)),
        out_shape=jax.ShapeDtypeStruct((m, PROJ_PAD), F32),
        scratch_shapes=[pltpu.VMEM((tm, D_MODEL), BF16)],
        compiler_params=_cparams(("parallel", "arbitrary")),
        name="inproj",
    )(x, g, w)


def _outproj_kernel(x_ref, oa_ref, ob_ref, oc_ref, od_ref, w_ref, g_ref, y_ref, *, final):
    acc = x_ref[...]
    for i, o_ref in enumerate((oa_ref, ob_ref, oc_ref, od_ref)):
        acc = acc + _dot(o_ref[...], w_ref[i * W_BRANCH:(i + 1) * W_BRANCH, :])
    if final:
        ms = jnp.mean(acc * acc, axis=-1, keepdims=True)
        acc = acc * lax.rsqrt(ms + EPS) * g_ref[...]
    y_ref[...] = acc


def _outproj(x, outs, w, g, l, tm, final):
    m = x.shape[0]
    row = lambda i: (i, 0)
    return pl.pallas_call(
        functools.partial(_outproj_kernel, final=final),
        grid=(m // tm,),
        in_specs=[pl.BlockSpec((tm, D_MODEL), row)]
        + [pl.BlockSpec((tm, W_BRANCH), row)] * 4
        + [_layer((D_MODEL, D_MODEL), l), pl.BlockSpec((1, D_MODEL), lambda i: (0, 0))],
        out_specs=pl.BlockSpec((tm, D_MODEL), row),
        out_shape=jax.ShapeDtypeStruct((m, D_MODEL), F32),
        compiler_params=_cparams(("parallel",)),
        name="outproj",
    )(x, *outs, w, g)


def _head_rmsnorm(o, gain, heads, width):
    parts = []
    for h in range(heads):
        oh = o[:, h * width:(h + 1) * width]
        ms = jnp.mean(oh * oh, axis=-1, keepdims=True)
        parts.append(oh * lax.rsqrt(ms + EPS))
    return jnp.concatenate(parts, axis=-1) * gain


def _sgu_layernorm(v, gain):
    parts = []
    for h in range(SGU_HEADS):
        vh = v[:, h * SGU_HD:(h + 1) * SGU_HD]
        vh = vh - jnp.mean(vh, axis=-1, keepdims=True)
        parts.append(vh * lax.rsqrt(jnp.mean(vh * vh, axis=-1, keepdims=True) + EPS))
    return jnp.concatenate(parts, axis=-1) * gain


def _gla_log_decay(alr, w_a2, b_a):
    z = _dot_split(alr, w_a2) + b_a
    return _log_sigmoid(z) * (1.0 / GLA_TAU)


def _ssm_in(taps, wb_ref, dot):
    re, im = [], []
    for s in range(SSM_SLABS):
        lhs = jnp.concatenate([tap[:, s * LANE:(s + 1) * LANE] for tap in taps], axis=1)
        r = dot(lhs, wb_ref[s, 0:len(taps) * LANE, :])
        re.append(r[:, :SLAB_STATE])
        im.append(r[:, SLAB_STATE:])
    return re, im


def _ssm_out(xr, xi, wc_ref, s, dot):
    return dot(xr, wc_ref[s, :SLAB_STATE, :]) + dot(xi, wc_ref[s, SLAB_STATE:, :])


def _ssm_gate(y, u, gate, d, glu_w, glu_b):
    gd = jax.nn.gelu(y + d * u)
    return gd * jax.nn.sigmoid(_dot(gd.astype(BF16), glu_w) + glu_b) * _silu(gate)


def _ssm_prep_kernel(lr_ref, li_ref, ldt_ref, brt_ref, bit_ref, pwr_ref, pwi_ref, bbr_ref, bbi_ref):
    lr, li = lr_ref[...], li_ref[...]
    dt = jnp.exp(ldt_ref[...])
    for row, power in enumerate(SSM_POWERS):
        mag = jnp.exp(lr * dt * float(power))
        pwr_ref[row] = mag * jnp.cos(li * dt * float(power))
        pwi_ref[row] = mag * jnp.sin(li * dt * float(power))
    ar, ai = pwr_ref[0], pwi_ref[0]
    den = lr * lr + li * li
    cr = ((ar - 1.0) * lr + ai * li) / den
    ci = (ai * lr - (ar - 1.0) * li) / den
    br, bi = brt_ref[...], bit_ref[...]
    tr = cr * br - ci * bi
    ti = cr * bi + ci * br
    bbr_ref[0] = tr
    bbi_ref[0] = ti
    for s in range(1, SSM_TAPS):
        tr, ti = ar * tr - ai * ti, ar * ti + ai * tr
        bbr_ref[s] = tr
        bbi_ref[s] = ti


def _ssm_matrices(lam_re, lam_im, log_dt, b_re, b_im, c_re, c_im):
    depth = lam_re.shape[0]
    dg, n, p = depth * SSM_GROUPS, SSM_N, SSM_GROUP_SIZE
    pw_shape = jax.ShapeDtypeStruct((len(SSM_POWERS), dg, 1, n), F32)
    bb_shape = jax.ShapeDtypeStruct((SSM_TAPS, dg, p, n), F32)
    pwr, pwi, bbr_t, bbi_t = pl.pallas_call(
        _ssm_prep_kernel, out_shape=(pw_shape, pw_shape, bb_shape, bb_shape), name="ssm_prep",
    )(lam_re.reshape(dg, 1, n), lam_im.reshape(dg, 1, n), log_dt.reshape(dg, 1, 1),
      b_re.transpose(0, 1, 3, 2).reshape(dg, p, n), b_im.transpose(0, 1, 3, 2).reshape(dg, p, n))
    gs = SSM_GROUPS // SSM_SLABS

    def spread(rows, rows_per_group, width):
        shape = (rows.shape[2], gs * width)
        row_group = lax.rem(lax.div(lax.broadcasted_iota(jnp.int32, shape, 0), jnp.int32(rows_per_group)),
                            jnp.int32(gs))
        lane_group = lax.div(lax.broadcasted_iota(jnp.int32, shape, 1), jnp.int32(width))
        return jnp.where(row_group == lane_group, jnp.tile(rows, (1, 1, 1, gs)), 0.0)

    def tap_rows(bb):
        return bb.reshape(SSM_TAPS, depth, SSM_SLABS, gs * p, n).transpose(1, 2, 0, 3, 4).reshape(
            depth, SSM_SLABS, SSM_TAPS * LANE, n)

    def state_rows(c):
        return c.transpose(0, 1, 3, 2).reshape(depth, SSM_SLABS, SLAB_STATE, p)

    wb = jnp.concatenate([spread(tap_rows(bbr_t), p, n), spread(tap_rows(bbi_t), p, n)], axis=-1)
    wc = jnp.concatenate([spread(state_rows(c_re), n, p), spread(state_rows(-c_im), n, p)], axis=2)
    powers = lambda pw: pw.reshape(len(SSM_POWERS), depth, SSM_STATE).transpose(1, 0, 2)
    return powers(pwr), powers(pwi), wb.astype(BF16), wc.astype(BF16)


def _gla_kernel(q_ref, k_ref, v_ref, ga_ref, alr_ref, wa2_ref, ba_ref, gain_ref, o_ref, s_ref, st_ref):
    c = GLA_CHUNK
    i = pl.program_id(1)

    @pl.when(i == 0)
    def _():
        st_ref[...] = jnp.zeros_like(st_ref)

    row = lax.broadcasted_iota(jnp.int32, (c, c), 0)
    col = lax.broadcasted_iota(jnp.int32, (c, c), 1)
    tril = jnp.where(col <= row, 1.0, 0.0).astype(BF16)
    key_head = lax.broadcasted_iota(jnp.int32, (c, GLA_HK), 1) // GLA_DK
    val_head = lax.broadcasted_iota(jnp.int32, (c, W_BRANCH), 1) // GLA_DV
    t_idx = lax.broadcasted_iota(jnp.int32, (c, GLA_HEADS * c), 0)
    s_idx = lax.broadcasted_iota(jnp.int32, (c, GLA_HEADS * c), 1) % c
    st_row_head = lax.broadcasted_iota(jnp.int32, st_ref.shape, 0) // GLA_DV
    st_col_head = lax.broadcasted_iota(jnp.int32, st_ref.shape, 1) // GLA_DK
    same_head = st_row_head == st_col_head

    key_masks = [jnp.where(key_head == h, 1.0, 0.0).astype(BF16) for h in range(GLA_HEADS)]
    val_masks = [jnp.where(val_head == h, 1.0, 0.0).astype(BF16) for h in range(GLA_HEADS)]
    chunks = [slice(r0, r0 + c) for r0 in range(0, q_ref.shape[0], c)]

    la = _gla_log_decay(alr_ref[...], wa2_ref[...], ba_ref[...])
    qes, kds, decays, scores = [], [], [], []
    for rows in chunks:
        la_hi, la_lo = _split(la[rows, :])
        b = _dot(tril, la_hi) + _dot(tril, la_lo)
        b_last = b[c - 1:c, :]
        k = k_ref[rows, :]
        qe = (q_ref[rows, :] * (GLA_DK ** -0.5) * jnp.exp(b)).astype(BF16)
        ke = (k * jnp.exp(jnp.minimum(-b, EXP_CLAMP))).astype(BF16)
        k_bd = jnp.concatenate([ke * m for m in key_masks], axis=0)
        sc = _dot_nt(qe, k_bd)
        scores.append(jnp.where(s_idx <= t_idx, sc, 0.0).astype(BF16))
        qes.append(qe)
        kds.append((k * jnp.exp(b_last - b)).astype(BF16))
        decays.append(jnp.exp(b_last))
    intra, upds = [], []
    for n, rows in enumerate(chunks):
        v = v_ref[rows, :]
        vb = v.astype(BF16)
        v_bd = jnp.concatenate([vb * m for m in val_masks], axis=0)
        intra.append(_dot(scores[n], v_bd))
        upds.append(jnp.where(same_head, _dot(v.T.astype(BF16), kds[n]), 0.0))
    st = st_ref[...]
    outs = []
    for n in range(len(chunks)):
        outs.append(intra[n] + _dot_nt(qes[n], st.astype(BF16)))
        st = st * decays[n] + upds[n]
    st_ref[...] = st
    for n, rows in enumerate(chunks):
        o = _head_rmsnorm(outs[n], gain_ref[...], GLA_HEADS, GLA_DV)
        o_ref[rows, :] = (o * _silu(ga_ref[rows, :])).astype(o_ref.dtype)

    @pl.when(i == pl.num_programs(1) - 1)
    def _():
        for h in range(GLA_HEADS):
            slab_t = st[h * GLA_DV:(h + 1) * GLA_DV, :].T
            s_ref[h] = slab_t[h * GLA_DK:(h + 1) * GLA_DK, :]


def _gla_prompt(p3, w_a2p, b_a, gain, l, prev):
    bsz, seq, _ = p3.shape
    depth = w_a2p.shape[0]
    c = GLA_TILE
    blk = lambda width, idx: pl.BlockSpec((None, c, width), lambda b, i: (b, i, idx))
    call, extra = _stacked_call(
        _gla_kernel, 8, prev, 1,
        grid=(bsz, seq // c),
        in_specs=[blk(GLA_HK, 0), blk(GLA_HK, 1), blk(W_BRANCH, COL_V), blk(W_BRANCH, COL_GA), blk(LANE, ALR_BLOCK),
                  _layer((LANE, GLA_HK), l), _layer((1, GLA_HK), l), _layer((1, W_BRANCH), l)],
        out_specs=[pl.BlockSpec((None, c, W_BRANCH), lambda b, i: (b, i, 0)),
                   pl.BlockSpec((None, None, GLA_HEADS, GLA_DK, GLA_DV), lambda b, i: (l, b, 0, 0, 0))],
        out_shape=(jax.ShapeDtypeStruct((bsz, seq, W_BRANCH), BF16),
                   jax.ShapeDtypeStruct((depth, bsz, GLA_HEADS, GLA_DK, GLA_DV), F32)),
        scratch_shapes=[pltpu.VMEM((GLA_HEADS * GLA_DV, GLA_HK), F32)],
        compiler_params=_cparams(("parallel", "arbitrary")),
        name="gla_prompt")
    return call(p3, p3, p3, p3, p3, w_a2p, b_a, gain, *extra)


def _sgu_kernel(u_ref, v_ref, g_ref, gain_ref, w_ref, bt_ref, o_ref):
    c = SGU_CHUNK
    row = lax.broadcasted_iota(jnp.int32, (c, c), 0)
    col = lax.broadcasted_iota(jnp.int32, (c, c), 1)
    w = [jnp.where(col <= row, w_ref[h], 0.0).astype(BF16) for h in range(SGU_HEADS)]
    for r0 in range(0, u_ref.shape[0], c):
        rows = slice(r0, r0 + c)
        v_n = _sgu_layernorm(v_ref[rows, :], gain_ref[...])
        parts = []
        for h in range(SGU_HEADS):
            mixed = _dot(w[h], v_n[:, h * SGU_HD:(h + 1) * SGU_HD].astype(BF16))
            parts.append(mixed + bt_ref[:, h:h + 1])
        mixed = jnp.concatenate(parts, axis=-1)
        o_ref[rows, :] = (u_ref[rows, :] * mixed * _silu(g_ref[rows, :])).astype(o_ref.dtype)


def _sgu_prompt(p3, gain, sgu_w, sgu_b_t, l):
    bsz, seq, _ = p3.shape
    c = SGU_CHUNK
    tile = SGU_TILE
    blk = lambda idx: pl.BlockSpec((None, tile, W_BRANCH), lambda b, i: (b, i, idx))
    return pl.pallas_call(
        _sgu_kernel,
        grid=(bsz, seq // tile),
        in_specs=[blk(COL_UB), blk(COL_VB), blk(COL_GB), _layer((1, W_BRANCH), l),
                  _layer((SGU_HEADS, c, c), l), _layer((c, SGU_HEADS), l)],
        out_specs=pl.BlockSpec((None, tile, W_BRANCH), lambda b, i: (b, i, 0)),
        out_shape=jax.ShapeDtypeStruct((bsz, seq, W_BRANCH), BF16),
        compiler_params=_cparams(("parallel", "parallel")),
        name="sgu_prompt",
    )(p3, p3, p3, gain, sgu_w, sgu_b_t)


def _conv_kernel(cb_ref, cc_ref, hc_ref, g_ref, w_ref, o_ref, new_ref, zbuf_ref):
    t = cb_ref.shape[0]
    i = pl.program_id(1)

    @pl.when(i == 0)
    def _():
        zbuf_ref[0:SUBLANE, :] = jnp.zeros((SUBLANE, W_BRANCH), F32)

    z = cc_ref[...] * hc_ref[...]
    zbuf_ref[SUBLANE:SUBLANE + t, :] = z
    w = w_ref[...]
    y = w[0:1, :] * zbuf_ref[SUBLANE - 2:SUBLANE - 2 + t, :]
    y = y + w[1:2, :] * zbuf_ref[SUBLANE - 1:SUBLANE - 1 + t, :]
    y = y + w[2:3, :] * z
    o_ref[...] = (cb_ref[...] * y * _silu(g_ref[...])).astype(o_ref.dtype)
    zbuf_ref[0:SUBLANE, :] = z[t - SUBLANE:t, :]

    @pl.when(i == pl.num_programs(1) - 1)
    def _():
        new_ref[:, 0:W_BRANCH] = z[t - 2:t - 1, :]
        new_ref[:, W_BRANCH:2 * W_BRANCH] = z[t - 1:t, :]


def _conv_prompt(p3, conv_w, l, prev):
    bsz, seq, _ = p3.shape
    depth = conv_w.shape[0]
    tile = CONV_TILE
    blk = lambda idx: pl.BlockSpec((None, tile, W_BRANCH), lambda b, i: (b, i, idx))
    call, extra = _stacked_call(
        _conv_kernel, 5, prev, 1,
        grid=(bsz, seq // tile),
        in_specs=[blk(COL_CB), blk(COL_CC), blk(COL_HC), blk(COL_GC), _layer((3, W_BRANCH), l)],
        out_specs=[pl.BlockSpec((None, tile, W_BRANCH), lambda b, i: (b, i, 0)),
                   pl.BlockSpec((None, None, 1, 2 * W_BRANCH), lambda b, i: (l, b, 0, 0))],
        out_shape=(jax.ShapeDtypeStruct((bsz, seq, W_BRANCH), BF16),
                   jax.ShapeDtypeStruct((depth, bsz, 1, 2 * W_BRANCH), F32)),
        scratch_shapes=[pltpu.VMEM((tile + SUBLANE, W_BRANCH), F32)],
        compiler_params=_cparams(("parallel", "arbitrary")),
        name="conv_prompt")
    return call(p3, p3, p3, p3, conv_w, *extra)


def _ssm_kernel(u_ref, g_ref, wb_ref, wc_ref, pwr_ref, pwi_ref, d_ref, gluw_ref, glub_ref,
                o_ref, sr_ref, si_ref, ubuf_ref, xr_ref, xi_ref, cr_ref, ci_ref, y_ref):
    t = u_ref.shape[0]
    i = pl.program_id(1)
    half = SUBLANE // 2

    @pl.when(i == 0)
    def _():
        ubuf_ref[0:SUBLANE, :] = jnp.zeros((SUBLANE, W_BRANCH), F32)
        cr_ref[...] = jnp.zeros_like(cr_ref)
        ci_ref[...] = jnp.zeros_like(ci_ref)

    u = u_ref[...]
    ubuf_ref[SUBLANE:SUBLANE + t, :] = u
    taps = [u.astype(BF16)]
    taps += [ubuf_ref[SUBLANE - s:SUBLANE - s + t, :].astype(BF16) for s in range(1, SSM_TAPS)]
    ubuf_ref[0:SUBLANE, :] = u[t - SUBLANE:t, :]
    re, im = _ssm_in(taps, wb_ref, _dot)
    for s in range(SSM_SLABS):
        xr_ref[:, s * SLAB_STATE:(s + 1) * SLAB_STATE] = re[s]
        xi_ref[:, s * SLAB_STATE:(s + 1) * SLAB_STATE] = im[s]

    row = lax.broadcasted_iota(jnp.int32, (SUBLANE, SLAB_STATE), 0)
    for s in range(SSM_SLABS):
        lanes = slice(s * SLAB_STATE, (s + 1) * SLAB_STATE)
        p4r, p4i = pwr_ref[1:2, lanes], pwi_ref[1:2, lanes]
        lo_r, lo_i = jnp.where(row < half, p4r, 0.0), jnp.where(row < half, p4i, 0.0)
        hi_r, hi_i = jnp.where(row >= half, p4r, 0.0), jnp.where(row >= half, p4i, 0.0)
        xr, xi = cr_ref[:, lanes], ci_ref[:, lanes]
        for j in range(t // SUBLANE):
            rows = slice(j * SUBLANE, (j + 1) * SUBLANE)
            qr, qi = pltpu.roll(xr, half, axis=0), pltpu.roll(xi, half, axis=0)
            yr = xr_ref[rows, lanes] + (lo_r * qr - lo_i * qi)
            yi = xi_ref[rows, lanes] + (lo_r * qi + lo_i * qr)
            qr, qi = pltpu.roll(yr, half, axis=0), pltpu.roll(yi, half, axis=0)
            xr = yr + (hi_r * qr - hi_i * qi)
            xi = yi + (hi_r * qi + hi_i * qr)
            xr_ref[rows, lanes] = xr
            xi_ref[rows, lanes] = xi
        cr_ref[:, lanes] = xr
        ci_ref[:, lanes] = xi
        y_ref[:, s * LANE:(s + 1) * LANE] = _ssm_out(xr_ref[:, lanes].astype(BF16), xi_ref[:, lanes].astype(BF16),
                                                     wc_ref, s, _dot)

    o = _ssm_gate(y_ref[...], u, g_ref[...], d_ref[...], gluw_ref[...], glub_ref[...])
    o_ref[...] = o.astype(o_ref.dtype)

    @pl.when(i == pl.num_programs(1) - 1)
    def _():
        sr_ref[...] = cr_ref[SUBLANE - 1:SUBLANE, :]
        si_ref[...] = ci_ref[SUBLANE - 1:SUBLANE, :]


def _ssm_prompt(p3, wb, wc, pwr, pwi, d, glu_w, glu_b, l, prev):
    bsz, seq, _ = p3.shape
    depth = wb.shape[0]
    tile = SSM_TILE
    blk = lambda idx: pl.BlockSpec((None, tile, W_BRANCH), lambda b, i: (b, i, idx))
    state = pl.BlockSpec((None, None, 1, SSM_STATE), lambda b, i: (l, b, 0, 0))
    state_shape = jax.ShapeDtypeStruct((depth, bsz, 1, SSM_STATE), F32)
    call, extra = _stacked_call(
        _ssm_kernel, 9, prev, 1,
        grid=(bsz, seq // tile),
        in_specs=[blk(COL_UD), blk(COL_GD), _layer(wb.shape[1:], l), _layer(wc.shape[1:], l),
                  _layer(pwr.shape[1:], l), _layer(pwi.shape[1:], l),
                  _layer((1, W_BRANCH), l), _layer((W_BRANCH, W_BRANCH), l), _layer((1, W_BRANCH), l)],
        out_specs=[pl.BlockSpec((None, tile, W_BRANCH), lambda b, i: (b, i, 0)), state, state],
        out_shape=(jax.ShapeDtypeStruct((bsz, seq, W_BRANCH), BF16), state_shape, state_shape),
        scratch_shapes=[pltpu.VMEM((tile + SUBLANE, W_BRANCH), F32),
                        pltpu.VMEM((tile, SSM_STATE), F32), pltpu.VMEM((tile, SSM_STATE), F32),
                        pltpu.VMEM((SUBLANE, SSM_STATE), F32), pltpu.VMEM((SUBLANE, SSM_STATE), F32),
                        pltpu.VMEM((tile, W_BRANCH), F32)],
        compiler_params=_cparams(("parallel", "arbitrary")),
        name="ssm_prompt")
    return call(p3, p3, wb, wc, pwr, pwi, d, glu_w, glu_b, *extra)


def _decay_kernel(alr_ref, q_ref, k_ref, wa2_ref, ba_ref, at_ref, qt_ref, kt_ref):
    at_ref[...] = jnp.exp(_gla_log_decay(alr_ref[...], wa2_ref[...], ba_ref[...])).T
    qt_ref[...] = (q_ref[...] * (GLA_DK ** -0.5)).T
    kt_ref[...] = k_ref[...].T


def _sample_decay(p, w_a2p, b_a, l):
    n = p.shape[0]
    out = jax.ShapeDtypeStruct((GLA_HK, n), F32)
    return pl.pallas_call(
        _decay_kernel,
        grid=(1,),
        in_specs=[pl.BlockSpec((n, LANE), lambda i: (0, ALR_BLOCK)), pl.BlockSpec((n, GLA_HK), lambda i: (0, 0)),
                  pl.BlockSpec((n, GLA_HK), lambda i: (0, 1)), _layer((LANE, GLA_HK), l), _layer((1, GLA_HK), l)],
        out_specs=[pl.BlockSpec((GLA_HK, n), lambda i: (0, 0))] * 3,
        out_shape=(out, out, out),
        name="sample_decay",
    )(p, p, p, w_a2p, b_a)


def _sample_kernel(at_ref, qt_ref, kt_ref, v_ref, ga_ref, ub_ref, vb_ref, gb_ref, cb_ref, cc_ref, hc_ref, gc_ref,
                   ud_ref, gd_ref, sgla_ref, sconv_ref, sre_ref, sim_ref,
                   glag_ref, sgug_ref, sguw_ref, sgub_ref, convw_ref,
                   wb_ref, wc_ref, pwr_ref, pwi_ref, d_ref, gluw_ref, glub_ref,
                   oa_ref, ob_ref, oc_ref, od_ref, ngla_ref, nconv_ref, nre_ref, nim_ref, vn_ref, orow_ref):
    rows = v_ref.shape[0]

    lanes = at_ref.shape[1]
    shift = (lanes - (pl.program_id(0) * rows) % lanes) % lanes
    a_t = pltpu.roll(at_ref[...], shift, axis=1)
    q_t = pltpu.roll(qt_ref[...], shift, axis=1)
    k_t = pltpu.roll(kt_ref[...], shift, axis=1)
    v = v_ref[...]
    for n in range(rows):
        for h in range(GLA_HEADS):
            keys = slice(h * GLA_DK, (h + 1) * GLA_DK)
            vals = slice(h * GLA_DV, (h + 1) * GLA_DV)
            s_new = a_t[keys, n:n + 1] * sgla_ref[n, h] + k_t[keys, n:n + 1] * v[n:n + 1, vals]
            ngla_ref[n, h] = s_new
            orow_ref[n:n + 1, vals] = jnp.sum(q_t[keys, n:n + 1] * s_new, axis=0, keepdims=True)
    o_a = _head_rmsnorm(orow_ref[...], glag_ref[...], GLA_HEADS, GLA_DV)
    oa_ref[...] = (o_a * _silu(ga_ref[...])).astype(oa_ref.dtype)

    v_n = _sgu_layernorm(vb_ref[...], sgug_ref[...])
    vn_ref[...] = v_n
    ob_ref[...] = (ub_ref[...] * (sguw_ref[...] * v_n + sgub_ref[...]) * _silu(gb_ref[...])).astype(ob_ref.dtype)

    z = cc_ref[...] * hc_ref[...]
    w = convw_ref[...]
    buf0, buf1 = sconv_ref[:, 0:W_BRANCH], sconv_ref[:, W_BRANCH:2 * W_BRANCH]
    y_c = w[0:1, :] * buf0 + w[1:2, :] * buf1 + w[2:3, :] * z
    oc_ref[...] = (cb_ref[...] * y_c * _silu(gc_ref[...])).astype(oc_ref.dtype)
    nconv_ref[:, 0:W_BRANCH] = buf1
    nconv_ref[:, W_BRANCH:2 * W_BRANCH] = z

    u = ud_ref[...]
    re, im = _ssm_in([u.astype(BF16)], wb_ref, _dot)
    ys = []
    for s in range(SSM_SLABS):
        lanes = slice(s * SLAB_STATE, (s + 1) * SLAB_STATE)
        ar, ai = pwr_ref[0:1, lanes], pwi_ref[0:1, lanes]
        x0r, x0i = sre_ref[:, lanes], sim_ref[:, lanes]
        xr = re[s] + (ar * x0r - ai * x0i)
        xi = im[s] + (ar * x0i + ai * x0r)
        nre_ref[:, lanes] = xr
        nim_ref[:, lanes] = xi
        ys.append(_ssm_out(xr.astype(BF16), xi.astype(BF16), wc_ref, s, _dot))
    y = jnp.concatenate(ys, axis=-1)
    od_ref[...] = _ssm_gate(y, u, gd_ref[...], d_ref[...], gluw_ref[...], glub_ref[...]).astype(od_ref.dtype)


def _sample_mixers(p, cols_aqk, states, weights, l, prev):
    n = p.shape[0]
    depth = states[0].shape[0]
    rows = SAMPLE_ROWS
    lanes = min(LANE, n)
    assert n % lanes == 0 and lanes % rows == 0
    cols = pl.BlockSpec((GLA_HK, lanes), lambda i: (0, (i * rows) // lanes))
    col = lambda idx: pl.BlockSpec((rows, W_BRANCH), lambda i: (i, idx))
    wide = lambda width: pl.BlockSpec((rows, width), lambda i: (i, 0))
    lwide = lambda width: pl.BlockSpec((None, rows, width), lambda i: (l, i, 0))
    gla_state = pl.BlockSpec((None, rows, GLA_HEADS, GLA_DK, GLA_DV), lambda i: (l, i, 0, 0, 0))
    state_specs = [gla_state, lwide(2 * W_BRANCH), lwide(SSM_STATE), lwide(SSM_STATE)]
    p_cols = (COL_V, COL_GA, COL_UB, COL_VB, COL_GB, COL_CB, COL_CC, COL_HC, COL_GC, COL_UD, COL_GD)
    out_bf = jax.ShapeDtypeStruct((n, W_BRANCH), BF16)
    n_in = 3 + len(p_cols) + len(states) + len(weights)
    call, extra = _stacked_call(
        _sample_kernel, n_in, prev, 4,
        grid=(n // rows,),
        in_specs=[cols] * 3 + [col(idx) for idx in p_cols] + state_specs
        + [_layer(w.shape[1:], l) for w in weights],
        out_specs=[wide(W_BRANCH)] * 4 + state_specs + [lwide(W_BRANCH)],
        out_shape=(out_bf, out_bf, out_bf, out_bf)
        + tuple(jax.ShapeDtypeStruct(s.shape, F32) for s in states)
        + (jax.ShapeDtypeStruct((depth, n, W_BRANCH), F32),),
        scratch_shapes=[pltpu.VMEM((rows, W_BRANCH), F32)],
        compiler_params=_cparams(("parallel",)),
        name="sample_mixers")
    return call(*cols_aqk, *([p] * len(p_cols)), *states, *weights, *extra)


def kernel(x_prompt, x_sample, state_gla, state_conv, state_ssm_re, state_ssm_im, norm_g, w_in, w_a2, b_a, gla_g,
           sgu_g, sgu_w, sgu_b, conv_w, ssm_lambda_re, ssm_lambda_im, ssm_log_dt, ssm_b_re, ssm_b_im, ssm_c_re,
           ssm_c_im, ssm_d, glu_w, glu_b, w_out, final_norm_g):
    bsz, seq, _ = x_prompt.shape
    nsamp, dec_seq, _ = x_sample.shape
    depth = w_in.shape[0]
    assert dec_seq == 1 and nsamp % SAMPLE_ROWS == 0 and (bsz * seq) % INPROJ_ROWS == 0
    assert all(seq % tile == 0 for tile in (GLA_TILE, SGU_TILE, CONV_TILE, SSM_TILE))
    hp = x_prompt.reshape(bsz * seq, D_MODEL)
    hs = x_sample.reshape(nsamp, D_MODEL)

    vec = lambda a: a.reshape(depth, 1, a.shape[-1])
    w_in_p = _prep_w_in(w_in)
    w_out_b = w_out.astype(BF16)
    glu_w_b = glu_w.astype(BF16)
    norm_g3, b_a3, gla_g3, sgu_g3, d3, glu_b3 = map(vec, (norm_g, b_a, gla_g, sgu_g, ssm_d, glu_b))
    final_g = final_norm_g.reshape(1, D_MODEL)
    w_a2p = jnp.pad(w_a2, ((0, 0), (0, LANE - GLA_RANK), (0, 0)))
    sgu_b_t = sgu_b.transpose(0, 2, 1)
    sgu_w0 = vec(jnp.repeat(sgu_w[:, :, 0, 0], SGU_HD, axis=-1))
    sgu_b0 = vec(jnp.repeat(sgu_b[:, :, 0], SGU_HD, axis=-1))
    pwr, pwi, wb, wc = _ssm_matrices(ssm_lambda_re, ssm_lambda_im, ssm_log_dt, ssm_b_re, ssm_b_im,
                                     ssm_c_re, ssm_c_im)
    sample_states = (state_gla, state_conv.reshape(depth, nsamp, 2 * W_BRANCH),
                     state_ssm_re.reshape(depth, nsamp, SSM_STATE), state_ssm_im.reshape(depth, nsamp, SSM_STATE))
    sample_weights = (gla_g3, sgu_g3, sgu_w0, sgu_b0, conv_w, wb, wc, pwr, pwi, d3, glu_w_b, glu_b3)

    gla_p = conv_p = ssm_p = samp = None
    for l in range(depth):
        final = l == depth - 1

        p3 = _inproj(hp, norm_g3, w_in_p, l, INPROJ_ROWS).reshape(bsz, seq, PROJ_PAD)
        o_a, *gla_p = _gla_prompt(p3, w_a2p, b_a3, gla_g3, l, gla_p)
        o_b = _sgu_prompt(p3, sgu_g3, sgu_w, sgu_b_t, l)
        o_c, *conv_p = _conv_prompt(p3, conv_w, l, conv_p)
        o_d, *ssm_p = _ssm_prompt(p3, wb, wc, pwr, pwi, d3, glu_w_b, glu_b3, l, ssm_p)
        mixed = [o.reshape(bsz * seq, W_BRANCH) for o in (o_a, o_b, o_c, o_d)]
        hp = _outproj(hp, mixed, w_out_b, final_g, l, OUTPROJ_ROWS, final)

        ps = _inproj(hs, norm_g3, w_in_p, l, nsamp)
        s_a, s_b, s_c, s_d, *samp = _sample_mixers(ps, _sample_decay(ps, w_a2p, b_a3, l), sample_states,
                                                   sample_weights, l, samp)
        hs = _outproj(hs, [s_a, s_b, s_c, s_d], w_out_b, final_g, l, nsamp, final)

    gla_s, conv_s, re_s, im_s, vn_s = samp
    groups = lambda s: s.reshape(depth, -1, SSM_GROUPS, SSM_N)
    return (hp.reshape(bsz, seq, D_MODEL), hs.reshape(nsamp, 1, D_MODEL),
            gla_p[0], gla_s,
            conv_p[0].reshape(depth, bsz, 2, W_BRANCH), conv_s.reshape(depth, nsamp, 2, W_BRANCH),
            groups(ssm_p[0]), groups(ssm_p[1]), groups(re_s), groups(im_s),
            vn_s.reshape(depth, nsamp, 1, W_BRANCH))
```

```python
import functools

import jax
import jax.numpy as jnp
from jax import lax
from jax.experimental import pallas as pl
from jax.experimental.pallas import tpu as pltpu

F32 = jnp.float32
BF16 = jnp.bfloat16
HIGHEST = lax.Precision.HIGHEST

D_MODEL = 2048
W_BRANCH = 512
EPS = 1e-6
GLA_HEADS = 4
GLA_DK = 64
GLA_DV = 128
GLA_HK = GLA_HEADS * GLA_DK
GLA_RANK = 16
GLA_TAU = 16.0
SGU_HEADS = 4
SGU_HD = 128
SGU_CHUNK = 128
SSM_GROUPS = 32
SSM_GROUP_SIZE = 16
SSM_N = 64
SSM_STATE = SSM_GROUPS * SSM_N
SSM_TAPS = 4
SSM_POWERS = (1, SSM_TAPS)
PROJ_TOTAL = 6160

LANE = 128
SUBLANE = 8
MXU_TILE = 256
PROJ_PAD = 6400
ALR_SRC = 2 * GLA_HK + W_BRANCH
ALR_DST = PROJ_TOTAL - GLA_RANK
ALR_BLOCK = ALR_DST // LANE
COL_V, COL_GA, COL_UB, COL_VB, COL_GB, COL_CB, COL_CC, COL_HC, COL_GC, COL_UD, COL_GD = range(1, 12)
SSM_SLABS = W_BRANCH // LANE
SLAB_STATE = SSM_STATE // SSM_SLABS
GLA_CHUNK = 128
EXP_CLAMP = 80.0
VMEM_LIMIT = 56 * 1024 * 1024

INPROJ_ROWS, INPROJ_COLS = 1024, 5 * MXU_TILE
OUTPROJ_ROWS = 512
WPREP_ROWS = MXU_TILE
GLA_TILE = 512
SGU_CONV_TILE = 512
SSM_TILE = 512
SSM_SUBTILE = 256
SAMPLE_ROWS = 16


def _cparams(sem):
    return pltpu.CompilerParams(dimension_semantics=sem, vmem_limit_bytes=VMEM_LIMIT)


def _silu(x):
    return x * jax.nn.sigmoid(x)


def _log_sigmoid(x):
    return jnp.minimum(x, 0.0) - jnp.log(1.0 + jnp.exp(-jnp.abs(x)))


def _dot(a, b):
    return jnp.dot(a, b, preferred_element_type=F32)


def _dot_bf16(a, b):
    return jnp.dot(a.astype(BF16), b.astype(BF16), preferred_element_type=F32)


def _dot_exact(a, b):
    return jnp.dot(a, b, preferred_element_type=F32, precision=HIGHEST)


def _split(x):
    hi = x.astype(BF16)
    return hi, (x - hi.astype(F32)).astype(BF16)


def _dot_split(a, b):
    a_hi, a_lo = _split(a)
    b_hi, b_lo = _split(b)
    return _dot(a_hi, b_hi) + (_dot(a_hi, b_lo) + _dot(a_lo, b_hi))


def _dot_nt(a, b):
    return lax.dot_general(a, b, (((1,), (1,)), ((), ())), preferred_element_type=F32)


def _layer(shape, l):
    return pl.BlockSpec((None,) + tuple(shape), lambda *_: (l,) + (0,) * len(shape))


def _drop_refs(fn, start, count):
    def wrapped(*refs):
        return fn(*refs[:start], *refs[start + count:])
    return wrapped


def _stacked_call(kernel_fn, n_in, prev, out_first, **kwargs):
    in_specs = list(kwargs.pop("in_specs"))
    aliases = {}
    if prev is not None:
        in_specs += [pl.BlockSpec(memory_space=pl.ANY)] * len(prev)
        aliases = {n_in + k: out_first + k for k in range(len(prev))}
        kernel_fn = _drop_refs(kernel_fn, n_in, len(prev))
    return pl.pallas_call(kernel_fn, in_specs=in_specs, input_output_aliases=aliases, **kwargs), \
        (() if prev is None else tuple(prev))


def _wprep_kernel(w_ref, o_ref):
    last = pl.program_id(1) == pl.num_programs(1) - 1

    @pl.when(jnp.logical_not(last))
    def _():
        o_ref[...] = w_ref[0].astype(BF16)

    @pl.when(last)
    def _():
        o_ref[...] = jnp.zeros_like(o_ref)
        o_ref[0:GLA_RANK, :] = w_ref[0, 0:GLA_RANK, :].astype(BF16)


def _prep_w_in(w_in):
    depth = w_in.shape[0]
    w_t = jnp.swapaxes(w_in, 1, 2)
    tr = WPREP_ROWS
    n_head = ALR_SRC // tr
    n_body = ALR_DST // tr

    def src_row(j):
        u = GLA_RANK
        return jnp.where(j < n_head, j * (tr // u), jnp.where(j < n_body, j * (tr // u) + 1, ALR_SRC // u)) * u

    return pl.pallas_call(
        _wprep_kernel,
        grid=(depth, PROJ_PAD // tr),
        in_specs=[pl.BlockSpec((pl.Element(1), pl.Element(tr), pl.Element(D_MODEL)),
                               lambda l, j: (l, src_row(j), 0))],
        out_specs=pl.BlockSpec((None, tr, D_MODEL), lambda l, j: (l, j, 0)),
        out_shape=jax.ShapeDtypeStruct((depth, PROJ_PAD, D_MODEL), BF16),
        compiler_params=_cparams(("parallel", "parallel")),
        name="prep_w_in",
    )(w_t)


def _inproj_kernel(x_ref, g_ref, w_ref, o_ref, h_ref):
    @pl.when(pl.program_id(1) == 0)
    def _():
        x = x_ref[...]
        ms = jnp.mean(x * x, axis=-1, keepdims=True)
        h_ref[...] = (x * lax.rsqrt(ms + EPS) * g_ref[...]).astype(BF16)

    o_ref[...] = _dot_nt(h_ref[...], w_ref[...])


def _inproj(x, g, w, l, tm):
    m = x.shape[0]
    tn = INPROJ_COLS
    return pl.pallas_call(
        _inproj_kernel,
        grid=(m // tm, PROJ_PAD // tn),
        in_specs=[pl.BlockSpec((tm, D_MODEL), lambda i, j: (i, 0)),
                  _layer((1, D_MODEL), l),
                  pl.BlockSpec((None, tn, D_MODEL), lambda i, j: (l, j, 0))],
        out_specs=pl.BlockSpec((tm, tn), lambda i, j: (i, j)),
        out_shape=jax.ShapeDtypeStruct((m, PROJ_PAD), F32),
        scratch_shapes=[pltpu.VMEM((tm, D_MODEL), BF16)],
        compiler_params=_cparams(("parallel", "arbitrary")),
        name="inproj",
    )(x, g, w)


def _outproj_kernel(x_ref, oa_ref, ob_ref, oc_ref, od_ref, w_ref, g_ref, y_ref, *, final):
    acc = x_ref[...]
    for i, o_ref in enumerate((oa_ref, ob_ref, oc_ref, od_ref)):
        acc = acc + _dot(o_ref[...], w_ref[i * W_BRANCH:(i + 1) * W_BRANCH, :])
    if final:
        ms = jnp.mean(acc * acc, axis=-1, keepdims=True)
        acc = acc * lax.rsqrt(ms + EPS) * g_ref[...]
    y_ref[...] = acc


def _outproj(x, outs, w, g, l, tm, final):
    m = x.shape[0]
    row = lambda i: (i, 0)
    return pl.pallas_call(
        functools.partial(_outproj_kernel, final=final),
        grid=(m // tm,),
        in_specs=[pl.BlockSpec((tm, D_MODEL), row)]
        + [pl.BlockSpec((tm, W_BRANCH), row)] * 4
        + [_layer((D_MODEL, D_MODEL), l), pl.BlockSpec((1, D_MODEL), lambda i: (0, 0))],
        out_specs=pl.BlockSpec((tm, D_MODEL), row),
        out_shape=jax.ShapeDtypeStruct((m, D_MODEL), F32),
        compiler_params=_cparams(("parallel",)),
        name="outproj",
    )(x, *outs, w, g)


def _head_rmsnorm(o, gain, heads, width):
    parts = []
    for h in range(heads):
        oh = o[:, h * width:(h + 1) * width]
        ms = jnp.mean(oh * oh, axis=-1, keepdims=True)
        parts.append(oh * lax.rsqrt(ms + EPS))
    return jnp.concatenate(parts, axis=-1) * gain


def _sgu_layernorm(v, gain):
    parts = []
    for h in range(SGU_HEADS):
        vh = v[:, h * SGU_HD:(h + 1) * SGU_HD]
        vh = vh - jnp.mean(vh, axis=-1, keepdims=True)
        parts.append(vh * lax.rsqrt(jnp.mean(vh * vh, axis=-1, keepdims=True) + EPS))
    return jnp.concatenate(parts, axis=-1) * gain


def _gla_log_decay(alr, w_a2, b_a):
    z = _dot_split(alr, w_a2) + b_a
    return _log_sigmoid(z) * (1.0 / GLA_TAU)


def _ssm_in(taps, wb_ref, dot):
    re, im = [], []
    for s in range(SSM_SLABS):
        lhs = jnp.concatenate([tap[:, s * LANE:(s + 1) * LANE] for tap in taps], axis=1)
        r = dot(lhs, wb_ref[s, 0:len(taps) * LANE, :])
        re.append(r[:, :SLAB_STATE])
        im.append(r[:, SLAB_STATE:])
    return re, im


def _ssm_out(xr, xi, wc_ref, s, dot):
    return dot(xr, wc_ref[s, :SLAB_STATE, :]) + dot(xi, wc_ref[s, SLAB_STATE:, :])


def _ssm_gate(y, u, gate, d, glu_w, glu_b):
    gd = jax.nn.gelu(y + d * u)
    return gd * jax.nn.sigmoid(_dot(gd.astype(BF16), glu_w) + glu_b) * _silu(gate)


def _ssm_prep_kernel(lr_ref, li_ref, ldt_ref, brt_ref, bit_ref, pwr_ref, pwi_ref, bbr_ref, bbi_ref):
    lr, li = lr_ref[...], li_ref[...]
    dt = jnp.exp(ldt_ref[...])
    for row, power in enumerate(SSM_POWERS):
        mag = jnp.exp(lr * dt * float(power))
        pwr_ref[row] = mag * jnp.cos(li * dt * float(power))
        pwi_ref[row] = mag * jnp.sin(li * dt * float(power))
    ar, ai = pwr_ref[0], pwi_ref[0]
    den = lr * lr + li * li
    cr = ((ar - 1.0) * lr + ai * li) / den
    ci = (ai * lr - (ar - 1.0) * li) / den
    br, bi = brt_ref[...], bit_ref[...]
    tr = cr * br - ci * bi
    ti = cr * bi + ci * br
    bbr_ref[0] = tr
    bbi_ref[0] = ti
    for s in range(1, SSM_TAPS):
        tr, ti = ar * tr - ai * ti, ar * ti + ai * tr
        bbr_ref[s] = tr
        bbi_ref[s] = ti


def _ssm_matrices(lam_re, lam_im, log_dt, b_re, b_im, c_re, c_im):
    depth = lam_re.shape[0]
    dg, n, p = depth * SSM_GROUPS, SSM_N, SSM_GROUP_SIZE
    pw_shape = jax.ShapeDtypeStruct((len(SSM_POWERS), dg, 1, n), F32)
    bb_shape = jax.ShapeDtypeStruct((SSM_TAPS, dg, p, n), F32)
    pwr, pwi, bbr_t, bbi_t = pl.pallas_call(
        _ssm_prep_kernel, out_shape=(pw_shape, pw_shape, bb_shape, bb_shape), name="ssm_prep",
    )(lam_re.reshape(dg, 1, n), lam_im.reshape(dg, 1, n), log_dt.reshape(dg, 1, 1),
      b_re.transpose(0, 1, 3, 2).reshape(dg, p, n), b_im.transpose(0, 1, 3, 2).reshape(dg, p, n))
    gs = SSM_GROUPS // SSM_SLABS

    def spread(rows, rows_per_group, width):
        shape = (rows.shape[2], gs * width)
        row_group = lax.rem(lax.div(lax.broadcasted_iota(jnp.int32, shape, 0), jnp.int32(rows_per_group)),
                            jnp.int32(gs))
        lane_group = lax.div(lax.broadcasted_iota(jnp.int32, shape, 1), jnp.int32(width))
        return jnp.where(row_group == lane_group, jnp.tile(rows, (1, 1, 1, gs)), 0.0)

    def tap_rows(bb):
        return bb.reshape(SSM_TAPS, depth, SSM_SLABS, gs * p, n).transpose(1, 2, 0, 3, 4).reshape(
            depth, SSM_SLABS, SSM_TAPS * LANE, n)

    def state_rows(c):
        return c.transpose(0, 1, 3, 2).reshape(depth, SSM_SLABS, SLAB_STATE, p)

    wb = jnp.concatenate([spread(tap_rows(bbr_t), p, n), spread(tap_rows(bbi_t), p, n)], axis=-1)
    wc = jnp.concatenate([spread(state_rows(c_re), n, p), spread(state_rows(-c_im), n, p)], axis=2)
    powers = lambda pw: pw.reshape(len(SSM_POWERS), depth, SSM_STATE).transpose(1, 0, 2)
    return powers(pwr), powers(pwi), wb.astype(BF16), wc.astype(BF16)


def _gla_kernel(q_ref, k_ref, v_ref, ga_ref, alr_ref, wa2_ref, ba_ref, gain_ref, o_ref, s_ref, st_ref):
    c = GLA_CHUNK
    i = pl.program_id(1)

    @pl.when(i == 0)
    def _():
        st_ref[...] = jnp.zeros_like(st_ref)

    row = lax.broadcasted_iota(jnp.int32, (c, c), 0)
    col = lax.broadcasted_iota(jnp.int32, (c, c), 1)
    tril = jnp.where(col <= row, 1.0, 0.0).astype(BF16)
    key_head = lax.broadcasted_iota(jnp.int32, (c, GLA_HK), 1) // GLA_DK
    val_head = lax.broadcasted_iota(jnp.int32, (c, W_BRANCH), 1) // GLA_DV
    t_idx = lax.broadcasted_iota(jnp.int32, (c, GLA_HEADS * c), 0)
    s_idx = lax.broadcasted_iota(jnp.int32, (c, GLA_HEADS * c), 1) % c
    st_row_head = lax.broadcasted_iota(jnp.int32, st_ref.shape, 0) // GLA_DV
    st_col_head = lax.broadcasted_iota(jnp.int32, st_ref.shape, 1) // GLA_DK
    same_head = st_row_head == st_col_head

    key_masks = [jnp.where(key_head == h, 1.0, 0.0).astype(BF16) for h in range(GLA_HEADS)]
    val_masks = [jnp.where(val_head == h, 1.0, 0.0).astype(BF16) for h in range(GLA_HEADS)]
    chunks = [slice(r0, r0 + c) for r0 in range(0, q_ref.shape[0], c)]

    la = _gla_log_decay(alr_ref[...], wa2_ref[...], ba_ref[...])
    qes, kds, decays, scores = [], [], [], []
    for rows in chunks:
        la_hi, la_lo = _split(la[rows, :])
        b = _dot(tril, la_hi) + _dot(tril, la_lo)
        b_last = b[c - 1:c, :]
        k = k_ref[rows, :]
        qe = (q_ref[rows, :] * (GLA_DK ** -0.5) * jnp.exp(b)).astype(BF16)
        ke = (k * jnp.exp(jnp.minimum(-b, EXP_CLAMP))).astype(BF16)
        k_bd = jnp.concatenate([ke * m for m in key_masks], axis=0)
        sc = _dot_nt(qe, k_bd)
        scores.append(jnp.where(s_idx <= t_idx, sc, 0.0).astype(BF16))
        qes.append(qe)
        kds.append((k * jnp.exp(b_last - b)).astype(BF16))
        decays.append(jnp.exp(b_last))
    intra, upds = [], []
    for n, rows in enumerate(chunks):
        v = v_ref[rows, :]
        vb = v.astype(BF16)
        v_bd = jnp.concatenate([vb * m for m in val_masks], axis=0)
        intra.append(_dot(scores[n], v_bd))
        upds.append(jnp.where(same_head, _dot(v.T.astype(BF16), kds[n]), 0.0))
    st = st_ref[...]
    outs = []
    for n in range(len(chunks)):
        outs.append(intra[n] + _dot_nt(qes[n], st.astype(BF16)))
        st = st * decays[n] + upds[n]
    st_ref[...] = st
    for n, rows in enumerate(chunks):
        o = _head_rmsnorm(outs[n], gain_ref[...], GLA_HEADS, GLA_DV)
        o_ref[rows, :] = (o * _silu(ga_ref[rows, :])).astype(o_ref.dtype)

    @pl.when(i == pl.num_programs(1) - 1)
    def _():
        for h in range(GLA_HEADS):
            slab_t = st[h * GLA_DV:(h + 1) * GLA_DV, :].T
            s_ref[h] = slab_t[h * GLA_DK:(h + 1) * GLA_DK, :]


def _gla_prompt(p3, w_a2p, b_a, gain, l, prev):
    bsz, seq, _ = p3.shape
    depth = w_a2p.shape[0]
    c = GLA_TILE
    blk = lambda width, idx: pl.BlockSpec((None, c, width), lambda b, i: (b, i, idx))
    call, extra = _stacked_call(
        _gla_kernel, 8, prev, 1,
        grid=(bsz, seq // c),
        in_specs=[blk(GLA_HK, 0), blk(GLA_HK, 1), blk(W_BRANCH, COL_V), blk(W_BRANCH, COL_GA), blk(LANE, ALR_BLOCK),
                  _layer((LANE, GLA_HK), l), _layer((1, GLA_HK), l), _layer((1, W_BRANCH), l)],
        out_specs=[pl.BlockSpec((None, c, W_BRANCH), lambda b, i: (b, i, 0)),
                   pl.BlockSpec((None, None, GLA_HEADS, GLA_DK, GLA_DV), lambda b, i: (l, b, 0, 0, 0))],
        out_shape=(jax.ShapeDtypeStruct((bsz, seq, W_BRANCH), BF16),
                   jax.ShapeDtypeStruct((depth, bsz, GLA_HEADS, GLA_DK, GLA_DV), F32)),
        scratch_shapes=[pltpu.VMEM((GLA_HEADS * GLA_DV, GLA_HK), F32)],
        compiler_params=_cparams(("parallel", "arbitrary")),
        name="gla_prompt")
    return call(p3, p3, p3, p3, p3, w_a2p, b_a, gain, *extra)


def _sgu_kernel(u_ref, v_ref, g_ref, gain_ref, w_ref, bt_ref, o_ref):
    c = SGU_CHUNK
    row = lax.broadcasted_iota(jnp.int32, (c, c), 0)
    col = lax.broadcasted_iota(jnp.int32, (c, c), 1)
    w = [jnp.where(col <= row, w_ref[h], 0.0).astype(BF16) for h in range(SGU_HEADS)]
    v_n = _sgu_layernorm(v_ref[...], gain_ref[...]).astype(BF16)
    for r0 in range(0, u_ref.shape[0], c):
        rows = slice(r0, r0 + c)
        parts = []
        for h in range(SGU_HEADS):
            mixed = _dot(w[h], v_n[rows, h * SGU_HD:(h + 1) * SGU_HD])
            parts.append(mixed + bt_ref[:, h:h + 1])
        mixed = jnp.concatenate(parts, axis=-1)
        o_ref[rows, :] = (u_ref[rows, :] * mixed * _silu(g_ref[rows, :])).astype(o_ref.dtype)


def _conv_kernel(cb_ref, cc_ref, hc_ref, g_ref, w_ref, o_ref, new_ref, zbuf_ref):
    t = cb_ref.shape[0]
    i = pl.program_id(1)

    @pl.when(i == 0)
    def _():
        zbuf_ref[0:SUBLANE, :] = jnp.zeros((SUBLANE, W_BRANCH), F32)

    z = cc_ref[...] * hc_ref[...]
    zbuf_ref[SUBLANE:SUBLANE + t, :] = z
    w = w_ref[...]
    y = w[0:1, :] * zbuf_ref[SUBLANE - 2:SUBLANE - 2 + t, :]
    y = y + w[1:2, :] * zbuf_ref[SUBLANE - 1:SUBLANE - 1 + t, :]
    y = y + w[2:3, :] * z
    o_ref[...] = (cb_ref[...] * y * _silu(g_ref[...])).astype(o_ref.dtype)
    zbuf_ref[0:SUBLANE, :] = z[t - SUBLANE:t, :]

    @pl.when(i == pl.num_programs(1) - 1)
    def _():
        new_ref[:, 0:W_BRANCH] = z[t - 2:t - 1, :]
        new_ref[:, W_BRANCH:2 * W_BRANCH] = z[t - 1:t, :]


def _sgu_conv_kernel(u_ref, v_ref, g_ref, gain_ref, w_ref, bt_ref, cb_ref, cc_ref, hc_ref, gc_ref, cw_ref,
                     ob_ref, oc_ref, new_ref, zbuf_ref):
    _sgu_kernel(u_ref, v_ref, g_ref, gain_ref, w_ref, bt_ref, ob_ref)
    _conv_kernel(cb_ref, cc_ref, hc_ref, gc_ref, cw_ref, oc_ref, new_ref, zbuf_ref)


def _sgu_conv_prompt(p3, gain, sgu_w, sgu_b_t, conv_w, l, prev):
    bsz, seq, _ = p3.shape
    depth = conv_w.shape[0]
    c = SGU_CHUNK
    tile = SGU_CONV_TILE
    blk = lambda idx: pl.BlockSpec((None, tile, W_BRANCH), lambda b, i: (b, i, idx))
    out = pl.BlockSpec((None, tile, W_BRANCH), lambda b, i: (b, i, 0))
    out_shape = jax.ShapeDtypeStruct((bsz, seq, W_BRANCH), BF16)
    call, extra = _stacked_call(
        _sgu_conv_kernel, 11, prev, 2,
        grid=(bsz, seq // tile),
        in_specs=[blk(COL_UB), blk(COL_VB), blk(COL_GB), _layer((1, W_BRANCH), l),
                  _layer((SGU_HEADS, c, c), l), _layer((c, SGU_HEADS), l),
                  blk(COL_CB), blk(COL_CC), blk(COL_HC), blk(COL_GC), _layer((3, W_BRANCH), l)],
        out_specs=[out, out, pl.BlockSpec((None, None, 1, 2 * W_BRANCH), lambda b, i: (l, b, 0, 0))],
        out_shape=(out_shape, out_shape, jax.ShapeDtypeStruct((depth, bsz, 1, 2 * W_BRANCH), F32)),
        scratch_shapes=[pltpu.VMEM((tile + SUBLANE, W_BRANCH), F32)],
        compiler_params=_cparams(("parallel", "arbitrary")),
        name="sgu_conv_prompt")
    return call(p3, p3, p3, gain, sgu_w, sgu_b_t, p3, p3, p3, p3, conv_w, *extra)


def _ssm_kernel(u_ref, g_ref, wb_ref, wc_ref, pwr_ref, pwi_ref, d_ref, gluw_ref, glub_ref,
                o_ref, sr_ref, si_ref, ubuf_ref, xr_ref, xi_ref, cr_ref, ci_ref, tab_ref):
    t = u_ref.shape[0]
    i = pl.program_id(1)
    half = SUBLANE // 2

    @pl.when(i == 0)
    def _():
        ubuf_ref[0:SUBLANE, :] = jnp.zeros((SUBLANE, W_BRANCH), F32)
        cr_ref[...] = jnp.zeros_like(cr_ref)
        ci_ref[...] = jnp.zeros_like(ci_ref)

    row = lax.broadcasted_iota(jnp.int32, (SUBLANE, SSM_STATE), 0)
    p4r, p4i = pwr_ref[1:2, :], pwi_ref[1:2, :]
    tab_ref[0] = jnp.where(row < half, p4r, 0.0)
    tab_ref[1] = jnp.where(row < half, p4i, 0.0)
    tab_ref[2] = jnp.where(row >= half, p4r, 0.0)
    tab_ref[3] = jnp.where(row >= half, p4i, 0.0)

    ubuf_ref[SUBLANE:SUBLANE + t, :] = u_ref[...]
    slabs = [slice(s * SLAB_STATE, (s + 1) * SLAB_STATE) for s in range(SSM_SLABS)]
    state = [(cr_ref[:, lanes], ci_ref[:, lanes]) for lanes in slabs]

    for r0 in range(0, t, SSM_SUBTILE):
        sub = slice(r0, r0 + SSM_SUBTILE)
        taps = [ubuf_ref[SUBLANE + r0 - s:SUBLANE + r0 - s + SSM_SUBTILE, :].astype(BF16) for s in range(SSM_TAPS)]
        re, im = _ssm_in(taps, wb_ref, _dot)
        for lanes, r, m in zip(slabs, re, im):
            xr_ref[sub, lanes] = r
            xi_ref[sub, lanes] = m
        for j in range(r0, r0 + SSM_SUBTILE, SUBLANE):
            rows = slice(j, j + SUBLANE)
            for n, lanes in enumerate(slabs):
                xr, xi = state[n]
                qr, qi = pltpu.roll(xr, half, axis=0), pltpu.roll(xi, half, axis=0)
                yr = xr_ref[rows, lanes] + (tab_ref[0, :, lanes] * qr - tab_ref[1, :, lanes] * qi)
                yi = xi_ref[rows, lanes] + (tab_ref[0, :, lanes] * qi + tab_ref[1, :, lanes] * qr)
                qr, qi = pltpu.roll(yr, half, axis=0), pltpu.roll(yi, half, axis=0)
                xr = yr + (tab_ref[2, :, lanes] * qr - tab_ref[3, :, lanes] * qi)
                xi = yi + (tab_ref[2, :, lanes] * qi + tab_ref[3, :, lanes] * qr)
                xr_ref[rows, lanes] = xr
                xi_ref[rows, lanes] = xi
                state[n] = (xr, xi)
        y = jnp.concatenate([_ssm_out(xr_ref[sub, lanes].astype(BF16), xi_ref[sub, lanes].astype(BF16), wc_ref, n, _dot)
                             for n, lanes in enumerate(slabs)], axis=-1)
        o = _ssm_gate(y, u_ref[sub, :], g_ref[sub, :], d_ref[...], gluw_ref[...], glub_ref[...])
        o_ref[sub, :] = o.astype(o_ref.dtype)

    ubuf_ref[0:SUBLANE, :] = u_ref[t - SUBLANE:t, :]
    for lanes, (xr, xi) in zip(slabs, state):
        cr_ref[:, lanes] = xr
        ci_ref[:, lanes] = xi

    @pl.when(i == pl.num_programs(1) - 1)
    def _():
        sr_ref[...] = cr_ref[SUBLANE - 1:SUBLANE, :]
        si_ref[...] = ci_ref[SUBLANE - 1:SUBLANE, :]


def _ssm_prompt(p3, wb, wc, pwr, pwi, d, glu_w, glu_b, l, prev):
    bsz, seq, _ = p3.shape
    depth = wb.shape[0]
    tile = SSM_TILE
    blk = lambda idx: pl.BlockSpec((None, tile, W_BRANCH), lambda b, i: (b, i, idx))
    state = pl.BlockSpec((None, None, 1, SSM_STATE), lambda b, i: (l, b, 0, 0))
    state_shape = jax.ShapeDtypeStruct((depth, bsz, 1, SSM_STATE), F32)
    call, extra = _stacked_call(
        _ssm_kernel, 9, prev, 1,
        grid=(bsz, seq // tile),
        in_specs=[blk(COL_UD), blk(COL_GD), _layer(wb.shape[1:], l), _layer(wc.shape[1:], l),
                  _layer(pwr.shape[1:], l), _layer(pwi.shape[1:], l),
                  _layer((1, W_BRANCH), l), _layer((W_BRANCH, W_BRANCH), l), _layer((1, W_BRANCH), l)],
        out_specs=[pl.BlockSpec((None, tile, W_BRANCH), lambda b, i: (b, i, 0)), state, state],
        out_shape=(jax.ShapeDtypeStruct((bsz, seq, W_BRANCH), BF16), state_shape, state_shape),
        scratch_shapes=[pltpu.VMEM((tile + SUBLANE, W_BRANCH), F32),
                        pltpu.VMEM((tile, SSM_STATE), F32), pltpu.VMEM((tile, SSM_STATE), F32),
                        pltpu.VMEM((SUBLANE, SSM_STATE), F32), pltpu.VMEM((SUBLANE, SSM_STATE), F32),
                        pltpu.VMEM((4, SUBLANE, SSM_STATE), F32)],
        compiler_params=_cparams(("parallel", "arbitrary")),
        name="ssm_prompt")
    return call(p3, p3, wb, wc, pwr, pwi, d, glu_w, glu_b, *extra)


def _decay_kernel(alr_ref, q_ref, k_ref, wa2_ref, ba_ref, at_ref, qt_ref, kt_ref):
    at_ref[...] = jnp.exp(_gla_log_decay(alr_ref[...], wa2_ref[...], ba_ref[...])).T
    qt_ref[...] = (q_ref[...] * (GLA_DK ** -0.5)).T
    kt_ref[...] = k_ref[...].T


def _sample_decay(p, w_a2p, b_a, l):
    n = p.shape[0]
    out = jax.ShapeDtypeStruct((GLA_HK, n), F32)
    return pl.pallas_call(
        _decay_kernel,
        grid=(1,),
        in_specs=[pl.BlockSpec((n, LANE), lambda i: (0, ALR_BLOCK)), pl.BlockSpec((n, GLA_HK), lambda i: (0, 0)),
                  pl.BlockSpec((n, GLA_HK), lambda i: (0, 1)), _layer((LANE, GLA_HK), l), _layer((1, GLA_HK), l)],
        out_specs=[pl.BlockSpec((GLA_HK, n), lambda i: (0, 0))] * 3,
        out_shape=(out, out, out),
        name="sample_decay",
    )(p, p, p, w_a2p, b_a)


def _sample_kernel(at_ref, qt_ref, kt_ref, v_ref, ga_ref, ub_ref, vb_ref, gb_ref, cb_ref, cc_ref, hc_ref, gc_ref,
                   ud_ref, gd_ref, sgla_ref, sconv_ref, sre_ref, sim_ref,
                   glag_ref, sgug_ref, sguw_ref, sgub_ref, convw_ref,
                   wb_ref, wc_ref, pwr_ref, pwi_ref, d_ref, gluw_ref, glub_ref,
                   oa_ref, ob_ref, oc_ref, od_ref, ngla_ref, nconv_ref, nre_ref, nim_ref, vn_ref, orow_ref):
    rows = v_ref.shape[0]

    lanes = at_ref.shape[1]
    shift = (lanes - (pl.program_id(0) * rows) % lanes) % lanes
    a_t = pltpu.roll(at_ref[...], shift, axis=1)
    q_t = pltpu.roll(qt_ref[...], shift, axis=1)
    k_t = pltpu.roll(kt_ref[...], shift, axis=1)
    v = v_ref[...]
    for n in range(rows):
        for h in range(GLA_HEADS):
            keys = slice(h * GLA_DK, (h + 1) * GLA_DK)
            vals = slice(h * GLA_DV, (h + 1) * GLA_DV)
            s_new = a_t[keys, n:n + 1] * sgla_ref[n, h] + k_t[keys, n:n + 1] * v[n:n + 1, vals]
            ngla_ref[n, h] = s_new
            orow_ref[n:n + 1, vals] = jnp.sum(q_t[keys, n:n + 1] * s_new, axis=0, keepdims=True)
    o_a = _head_rmsnorm(orow_ref[...], glag_ref[...], GLA_HEADS, GLA_DV)
    oa_ref[...] = (o_a * _silu(ga_ref[...])).astype(oa_ref.dtype)

    v_n = _sgu_layernorm(vb_ref[...], sgug_ref[...])
    vn_ref[...] = v_n
    ob_ref[...] = (ub_ref[...] * (sguw_ref[...] * v_n + sgub_ref[...]) * _silu(gb_ref[...])).astype(ob_ref.dtype)

    z = cc_ref[...] * hc_ref[...]
    w = convw_ref[...]
    buf0, buf1 = sconv_ref[:, 0:W_BRANCH], sconv_ref[:, W_BRANCH:2 * W_BRANCH]
    y_c = w[0:1, :] * buf0 + w[1:2, :] * buf1 + w[2:3, :] * z
    oc_ref[...] = (cb_ref[...] * y_c * _silu(gc_ref[...])).astype(oc_ref.dtype)
    nconv_ref[:, 0:W_BRANCH] = buf1
    nconv_ref[:, W_BRANCH:2 * W_BRANCH] = z

    u = ud_ref[...]
    re, im = _ssm_in([u.astype(BF16)], wb_ref, _dot)
    ys = []
    for s in range(SSM_SLABS):
        lanes = slice(s * SLAB_STATE, (s + 1) * SLAB_STATE)
        ar, ai = pwr_ref[0:1, lanes], pwi_ref[0:1, lanes]
        x0r, x0i = sre_ref[:, lanes], sim_ref[:, lanes]
        xr = re[s] + (ar * x0r - ai * x0i)
        xi = im[s] + (ar * x0i + ai * x0r)
        nre_ref[:, lanes] = xr
        nim_ref[:, lanes] = xi
        ys.append(_ssm_out(xr.astype(BF16), xi.astype(BF16), wc_ref, s, _dot))
    y = jnp.concatenate(ys, axis=-1)
    od_ref[...] = _ssm_gate(y, u, gd_ref[...], d_ref[...], gluw_ref[...], glub_ref[...]).astype(od_ref.dtype)


def _sample_mixers(p, cols_aqk, states, weights, l, prev):
    n = p.shape[0]
    depth = states[0].shape[0]
    rows = SAMPLE_ROWS
    lanes = min(LANE, n)
    assert n % lanes == 0 and lanes % rows == 0
    cols = pl.BlockSpec((GLA_HK, lanes), lambda i: (0, (i * rows) // lanes))
    col = lambda idx: pl.BlockSpec((rows, W_BRANCH), lambda i: (i, idx))
    wide = lambda width: pl.BlockSpec((rows, width), lambda i: (i, 0))
    lwide = lambda width: pl.BlockSpec((None, rows, width), lambda i: (l, i, 0))
    gla_state = pl.BlockSpec((None, rows, GLA_HEADS, GLA_DK, GLA_DV), lambda i: (l, i, 0, 0, 0))
    state_specs = [gla_state, lwide(2 * W_BRANCH), lwide(SSM_STATE), lwide(SSM_STATE)]
    p_cols = (COL_V, COL_GA, COL_UB, COL_VB, COL_GB, COL_CB, COL_CC, COL_HC, COL_GC, COL_UD, COL_GD)
    out_bf = jax.ShapeDtypeStruct((n, W_BRANCH), BF16)
    n_in = 3 + len(p_cols) + len(states) + len(weights)
    call, extra = _stacked_call(
        _sample_kernel, n_in, prev, 4,
        grid=(n // rows,),
        in_specs=[cols] * 3 + [col(idx) for idx in p_cols] + state_specs
        + [_layer(w.shape[1:], l) for w in weights],
        out_specs=[wide(W_BRANCH)] * 4 + state_specs + [lwide(W_BRANCH)],
        out_shape=(out_bf, out_bf, out_bf, out_bf)
        + tuple(jax.ShapeDtypeStruct(s.shape, F32) for s in states)
        + (jax.ShapeDtypeStruct((depth, n, W_BRANCH), F32),),
        scratch_shapes=[pltpu.VMEM((rows, W_BRANCH), F32)],
        compiler_params=_cparams(("parallel",)),
        name="sample_mixers")
    return call(*cols_aqk, *([p] * len(p_cols)), *states, *weights, *extra)


def kernel(x_prompt, x_sample, state_gla, state_conv, state_ssm_re, state_ssm_im, norm_g, w_in, w_a2, b_a, gla_g,
           sgu_g, sgu_w, sgu_b, conv_w, ssm_lambda_re, ssm_lambda_im, ssm_log_dt, ssm_b_re, ssm_b_im, ssm_c_re,
           ssm_c_im, ssm_d, glu_w, glu_b, w_out, final_norm_g):
    bsz, seq, _ = x_prompt.shape
    nsamp, dec_seq, _ = x_sample.shape
    depth = w_in.shape[0]
    assert dec_seq == 1 and nsamp % SAMPLE_ROWS == 0 and (bsz * seq) % INPROJ_ROWS == 0
    assert all(seq % tile == 0 for tile in (GLA_TILE, SGU_CONV_TILE, SSM_TILE))
    hp = x_prompt.reshape(bsz * seq, D_MODEL)
    hs = x_sample.reshape(nsamp, D_MODEL)

    vec = lambda a: a.reshape(depth, 1, a.shape[-1])
    w_in_p = _prep_w_in(w_in)
    w_out_b = w_out.astype(BF16)
    glu_w_b = glu_w.astype(BF16)
    norm_g3, b_a3, gla_g3, sgu_g3, d3, glu_b3 = map(vec, (norm_g, b_a, gla_g, sgu_g, ssm_d, glu_b))
    final_g = final_norm_g.reshape(1, D_MODEL)
    w_a2p = jnp.pad(w_a2, ((0, 0), (0, LANE - GLA_RANK), (0, 0)))
    sgu_b_t = sgu_b.transpose(0, 2, 1)
    sgu_w0 = vec(jnp.repeat(sgu_w[:, :, 0, 0], SGU_HD, axis=-1))
    sgu_b0 = vec(jnp.repeat(sgu_b[:, :, 0], SGU_HD, axis=-1))
    pwr, pwi, wb, wc = _ssm_matrices(ssm_lambda_re, ssm_lambda_im, ssm_log_dt, ssm_b_re, ssm_b_im,
                                     ssm_c_re, ssm_c_im)
    sample_states = (state_gla, state_conv.reshape(depth, nsamp, 2 * W_BRANCH),
                     state_ssm_re.reshape(depth, nsamp, SSM_STATE), state_ssm_im.reshape(depth, nsamp, SSM_STATE))
    sample_weights = (gla_g3, sgu_g3, sgu_w0, sgu_b0, conv_w, wb, wc, pwr, pwi, d3, glu_w_b, glu_b3)

    gla_p = conv_p = ssm_p = samp = None
    for l in range(depth):
        final = l == depth - 1

        p3 = _inproj(hp, norm_g3, w_in_p, l, INPROJ_ROWS).reshape(bsz, seq, PROJ_PAD)
        o_a, *gla_p = _gla_prompt(p3, w_a2p, b_a3, gla_g3, l, gla_p)
        o_b, o_c, *conv_p = _sgu_conv_prompt(p3, sgu_g3, sgu_w, sgu_b_t, conv_w, l, conv_p)
        o_d, *ssm_p = _ssm_prompt(p3, wb, wc, pwr, pwi, d3, glu_w_b, glu_b3, l, ssm_p)
        mixed = [o.reshape(bsz * seq, W_BRANCH) for o in (o_a, o_b, o_c, o_d)]
        hp = _outproj(hp, mixed, w_out_b, final_g, l, OUTPROJ_ROWS, final)

        ps = _inproj(hs, norm_g3, w_in_p, l, nsamp)
        s_a, s_b, s_c, s_d, *samp = _sample_mixers(ps, _sample_decay(ps, w_a2p, b_a3, l), sample_states,
                                                   sample_weights, l, samp)
        hs = _outproj(hs, [s_a, s_b, s_c, s_d], w_out_b, final_g, l, nsamp, final)

    gla_s, conv_s, re_s, im_s, vn_s = samp
    groups = lambda s: s.reshape(depth, -1, SSM_GROUPS, SSM_N)
    return (hp.reshape(bsz, seq, D_MODEL), hs.reshape(nsamp, 1, D_MODEL),
            gla_p[0], gla_s,
            conv_p[0].reshape(depth, bsz, 2, W_BRANCH), conv_s.reshape(depth, nsamp, 2, W_BRANCH),
            groups(ssm_p[0]), groups(ssm_p[1]), groups(re_s), groups(im_s),
            vn_s.reshape(depth, nsamp, 1, W_BRANCH))
```
